```python
import math
import jax, jax.numpy as jnp
from jax import lax
import numpy as np

D_MODEL = 2048
BATCH = 2
SEQ = 8192
DEPTH = 1

N_META = 16
CONV_W = D_MODEL // 2
N_CONV_GROUPS = 16
CONV_K = 3
RWKV_W = D_MODEL // 2
HEAD_N = 64
N_RWKV_HEADS = RWKV_W // HEAD_N
DECAY_LORA = 96
AAA_LORA = 96
GATE_LORA = 256
D_FF = 5632
FFN_K = 3
RMS_EPS = 1e-6
GN_EPS = 64e-5
DECAY_SCALE = math.exp(-0.5)

CONV_COLS = 3 * CONV_W
RWKV_COLS = 3 * RWKV_W + DECAY_LORA + AAA_LORA + GATE_LORA
IN_COLS = CONV_COLS + RWKV_COLS

kernel_name = "hymba_conv_rwkv7_hybrid_layer"


def rmsnorm(x, g):
    xf = x.astype(jnp.float32)
    y = xf * lax.rsqrt(jnp.mean(xf * xf, axis=-1, keepdims=True) + RMS_EPS)
    return (y * g.astype(jnp.float32)).astype(x.dtype)


def causal_dwconv(x, w):
    k = w.shape[0]
    seq_len = x.shape[1]
    xp = jnp.pad(x, ((0, 0), (k - 1, 0), (0, 0)))
    y = xp[:, 0:seq_len] * w[0]
    for i in range(1, k):
        y = y + xp[:, i:i + seq_len] * w[i]
    return y


def token_shift(z, mu):
    z_prev = jnp.pad(z, ((0, 0), (1, 0), (0, 0)))[:, :-1]
    return z + (z_prev - z) * mu


def rwkv7_scan(r, w, k, v, a_vec, b_vec):
    bsz, _, h, n = r.shape
    xs = tuple(jnp.swapaxes(t, 0, 1) for t in (r, w, k, v, a_vec, b_vec))

    def step(s, inp):
        r_t, w_t, k_t, v_t, a_t, b_t = inp
        sa = jnp.einsum('bhvk,bhk->bhv', s, a_t)
        s = s * w_t[:, :, None, :] + sa[..., None] * b_t[:, :, None, :] + v_t[..., None] * k_t[:, :, None, :]
        y_t = jnp.einsum('bhvk,bhk->bhv', s, r_t)
        return s, y_t

    s0 = jnp.zeros((bsz, h, n, n), jnp.float32)
    _, ys = lax.scan(step, s0, xs)
    return jnp.swapaxes(ys, 0, 1)


def short_conv_group(zc, conv_w):
    gate_b, gate_c, hx = jnp.split(zc, 3, axis=-1)
    return gate_b * causal_dwconv(gate_c * hx, conv_w)


def rwkv7_group(zr, mu_rwkv, w_decay_up, w0, a_up, a0, g_up, k_k, k_a, r_k, ln_x_w, ln_x_b):
    bsz, seq_len, _ = zr.shape
    zr = token_shift(zr, mu_rwkv)
    o1 = RWKV_W; o2 = 2 * RWKV_W; o3 = 3 * RWKV_W
    o4 = o3 + DECAY_LORA; o5 = o4 + AAA_LORA
    r = zr[..., :o1]
    k = zr[..., o1:o2]
    v = zr[..., o2:o3]
    wd = zr[..., o3:o4]
    ad = zr[..., o4:o5]
    gd = zr[..., o5:]

    f32 = jnp.float32
    log_w = -DECAY_SCALE * jax.nn.sigmoid((w0 + jnp.tanh(wd) @ w_decay_up).astype(f32))
    a = jax.nn.sigmoid(a0 + ad @ a_up)
    g = jax.nn.sigmoid(gd) @ g_up

    hs = (bsz, seq_len, N_RWKV_HEADS, HEAD_N)
    kk = (k * k_k).astype(f32).reshape(hs)
    kk = kk / jnp.maximum(jnp.sqrt(jnp.sum(kk * kk, axis=-1, keepdims=True)), 1e-12)
    k = k * (1.0 + (a - 1.0) * k_a)

    r_h = r.astype(f32).reshape(hs)
    k_h = k.astype(f32).reshape(hs)
    v_h = v.astype(f32).reshape(hs)
    a_h = a.astype(f32).reshape(hs)
    y = rwkv7_scan(r_h, jnp.exp(log_w).reshape(hs), k_h, v_h, -kk, kk * a_h)

    mean = jnp.mean(y, axis=-1, keepdims=True)
    var = jnp.mean(jnp.square(y - mean), axis=-1, keepdims=True)
    y = (y - mean) * lax.rsqrt(var + GN_EPS)
    y = y.reshape(bsz, seq_len, RWKV_W) * ln_x_w.astype(f32) + ln_x_b.astype(f32)
    bonus = jnp.sum(r_h * k_h * r_k.astype(f32), axis=-1, keepdims=True) * v_h
    y = y + bonus.reshape(bsz, seq_len, RWKV_W)
    return (y * g.astype(f32)).astype(zr.dtype)


def conv_glu(x, w_gate_up, conv_ffn, w_down):
    gu = x @ w_gate_up
    gate, up = jnp.split(gu, 2, axis=-1)
    gate = causal_dwconv(gate, conv_ffn)
    return (jax.nn.silu(gate) * up) @ w_down


def setup_inputs(seed: int = 0) -> dict:
    key = jax.random.key(seed)
    ks = jax.random.split(key, 24)
    f32 = jnp.float32
    nrm = lambda k, shape, s: jax.random.normal(k, shape, f32) * s
    d = D_MODEL
    return {
        "x": nrm(ks[0], (BATCH, SEQ, d), 1.0),
        "meta_tokens": nrm(ks[1], (N_META, d), 1.0),
        "g_pre_mix": 1.0 + nrm(ks[2], (d,), 0.05),
        "w_in": nrm(ks[3], (d, IN_COLS), d ** -0.5),
        "conv_w": nrm(ks[4], (CONV_K, CONV_W), CONV_K ** -0.5),
        "mu_rwkv": jax.random.uniform(ks[5], (RWKV_COLS,), f32, 0.1, 0.9),
        "w_decay_up": nrm(ks[6], (DECAY_LORA, RWKV_W), 0.1 * DECAY_LORA ** -0.5),
        "w0": nrm(ks[7], (RWKV_W,), 0.5) - 0.5,
        "a_up": nrm(ks[8], (AAA_LORA, RWKV_W), 0.1 * AAA_LORA ** -0.5),
        "a0": nrm(ks[9], (RWKV_W,), 0.1),
        "g_up": nrm(ks[10], (GATE_LORA, RWKV_W), GATE_LORA ** -0.5),
        "k_k": 0.85 + nrm(ks[11], (RWKV_W,), 0.05),
        "k_a": 1.0 + nrm(ks[12], (RWKV_W,), 0.05),
        "r_k": nrm(ks[13], (N_RWKV_HEADS, HEAD_N), 0.1),
        "ln_x_w": 1.0 + nrm(ks[14], (RWKV_W,), 0.05),
        "ln_x_b": nrm(ks[15], (RWKV_W,), 0.02),
        "w_out": nrm(ks[16], (CONV_W + RWKV_W, d), (CONV_W + RWKV_W) ** -0.5),
        "g_post_mix": 1.0 + nrm(ks[17], (d,), 0.05),
        "g_pre_ffn": 1.0 + nrm(ks[18], (d,), 0.05),
        "w_gate_up": nrm(ks[19], (d, 2 * D_FF), d ** -0.5),
        "conv_ffn": nrm(ks[20], (FFN_K, D_FF), FFN_K ** -0.5),
        "w_down": nrm(ks[21], (D_FF, d), D_FF ** -0.5),
        "g_post_ffn": 1.0 + nrm(ks[22], (d,), 0.05),
    }


def reference(x, meta_tokens, g_pre_mix, w_in, conv_w, mu_rwkv, w_decay_up, w0, a_up, a0,
              g_up, k_k, k_a, r_k, ln_x_w, ln_x_b, w_out, g_post_mix, g_pre_ffn,
              w_gate_up, conv_ffn, w_down, g_post_ffn):
    bsz = x.shape[0]
    meta = jnp.broadcast_to(meta_tokens.astype(x.dtype)[None], (bsz, N_META, x.shape[-1]))
    h = jnp.concatenate([meta, x], axis=1)

    for _ in range(DEPTH):
        n = rmsnorm(h, g_pre_mix)
        z = n @ w_in
        y_conv = short_conv_group(z[..., :CONV_COLS], conv_w)
        y_rwkv = rwkv7_group(z[..., CONV_COLS:], mu_rwkv, w_decay_up, w0, a_up, a0, g_up,
                             k_k, k_a, r_k, ln_x_w, ln_x_b)
        mix = jnp.concatenate([y_conv, y_rwkv], axis=-1) @ w_out
        h = h + rmsnorm(mix, g_post_mix)
        f = conv_glu(rmsnorm(h, g_pre_ffn), w_gate_up, conv_ffn, w_down)
        h = h + rmsnorm(f, g_post_ffn)

    return h[:, N_META:]
```

```python
import functools
import math

import jax
import jax.numpy as jnp
from jax import lax
from jax.experimental import pallas as pl
from jax.experimental.pallas import tpu as pltpu

F32 = jnp.float32
BF16 = jnp.bfloat16

D_MODEL = 2048
SEQ = 8192
N_META = 16
GROUP_W = 1024
HEAD_N = 64
LORA_W = 96 + 96 + 256
LORA_PAD = 512
IN_COLS_PAD = 3 * GROUP_W + 3 * GROUP_W + LORA_PAD
D_FF = 5632
RMS_EPS = 1e-6
GN_EPS = 64e-5
DECAY_SCALE = math.exp(-0.5)

CHUNK = 64
PAD_FRONT = CHUNK - N_META
L_PAD = PAD_FRONT + N_META + SEQ
N_CHUNKS = L_PAD // CHUNK
HEADS_PER_GROUP = 4
SCAN_W = HEADS_PER_GROUP * HEAD_N

VMEM_LIMIT = 56 * 1024 * 1024

TM_PROJ = 1376
TN_PROJ = 512
TM_PREP = 192
TM_OUT = 688
TM_UP = 1376
TN_UP = 512
TM_DOWN = 688
TK_DOWN = 512


def _dot(a, b):
    return jnp.dot(a, b, preferred_element_type=F32)


def _dot_nt(a, b):
    return lax.dot_general(a, b, (((1,), (1,)), ((), ())), preferred_element_type=F32)


def _dot_tn(a, b):
    return lax.dot_general(a, b, (((0,), (0,)), ((), ())), preferred_element_type=F32)


def _rms(x, g):
    return x * lax.rsqrt(jnp.mean(x * x, axis=-1, keepdims=True) + RMS_EPS) * g


def _params(sem):
    return pltpu.CompilerParams(dimension_semantics=sem, vmem_limit_bytes=VMEM_LIMIT)


def _inproj_kernel(h_ref, g_ref, w_ref, z_ref, xn_ref):
    @pl.when(pl.program_id(1) == 0)
    def _():
        xn_ref[...] = _rms(h_ref[...], g_ref[...]).astype(BF16)

    z_ref[...] = _dot(xn_ref[...], w_ref[...])


def _inproj(h2d, g, w):
    m = h2d.shape[0]
    return pl.pallas_call(
        _inproj_kernel,
        grid=(m // TM_PROJ, IN_COLS_PAD // TN_PROJ),
        in_specs=[
            pl.BlockSpec((TM_PROJ, D_MODEL), lambda i, j: (i, 0)),
            pl.BlockSpec((1, D_MODEL), lambda i, j: (0, 0)),
            pl.BlockSpec((D_MODEL, TN_PROJ), lambda i, j: (0, j)),
        ],
        out_specs=pl.BlockSpec((TM_PROJ, TN_PROJ), lambda i, j: (i, j)),
        out_shape=jax.ShapeDtypeStruct((m, IN_COLS_PAD), F32),
        scratch_shapes=[pltpu.VMEM((TM_PROJ, D_MODEL), BF16)],
        compiler_params=_params(("parallel", "arbitrary")),
        name="inproj",
    )(h2d, g, w)


def _prev_rows(cur, halo, first, shift):
    halo = jnp.where(first, 0.0, halo)
    xs = jnp.concatenate([halo, cur], axis=0)
    return pltpu.roll(xs, shift, 0)[8:]


def _split_dot(m01, x):
    x1 = x.astype(BF16)
    r1 = x - x1.astype(F32)
    x2 = r1.astype(BF16)
    x3 = (r1 - x2.astype(F32)).astype(BF16)
    return _dot(m01, x1) + _dot(m01, x2) + _dot(m01, x3)


def _prep_kernel(zc_ref, zr_ref, zl_ref, hc_ref, hr_ref, hl_ref,
                 convw_ref, mur_ref, mul_ref, w0_ref, a0_ref, kk_ref, ka_ref, rk_ref,
                 wdec_ref, aup_ref, gup_ref, ones_ref, tri_ref, blk_ref,
                 yconv_o, at_o, rt_o, bt_o, kt_o, v_o, bh_o, kh_o, w_o, bonus_o, gate_o):
    first = pl.program_id(1) == 0
    g = GROUP_W

    zc = zc_ref[...]
    hc = hc_ref[...]
    p = zc[:, g:2 * g] * zc[:, 2 * g:]
    ph = hc[:, g:2 * g] * hc[:, 2 * g:]
    cw = convw_ref[...]
    conv = (_prev_rows(p, ph, first, 2) * cw[0:1] + _prev_rows(p, ph, first, 1) * cw[1:2]
            + p * cw[2:3])
    yconv_o[...] = (zc[:, :g] * conv).astype(BF16)

    zr = zr_ref[...]
    zs = zr + (_prev_rows(zr, hr_ref[...], first, 1) - zr) * mur_ref[...]
    zl = zl_ref[...]
    zsl = zl + (_prev_rows(zl, hl_ref[...], first, 1) - zl) * mul_ref[...]
    r = zs[:, :g]
    k = zs[:, g:2 * g]
    v = zs[:, 2 * g:]

    log_w = -DECAY_SCALE * jax.nn.sigmoid(w0_ref[...] + _dot(jnp.tanh(zsl).astype(BF16), wdec_ref[...]))
    a = jax.nn.sigmoid(a0_ref[...] + _dot(zsl.astype(BF16), aup_ref[...]))
    gate_o[...] = _dot(jax.nn.sigmoid(zsl).astype(BF16), gup_ref[...])

    ones_bd = ones_ref[...]
    kk = k * kk_ref[...]
    kk = kk / jnp.maximum(jnp.sqrt(_dot((kk * kk).astype(BF16), ones_bd)), 1e-12)
    k2 = k * (1.0 + (a - 1.0) * ka_ref[...])
    bonus_o[...] = _dot((r * k2 * rk_ref[...]).astype(BF16), ones_bd) * v
    bv = kk * a

    cum = _split_dot(tri_ref[...], log_w)
    tot = _split_dot(blk_ref[...], log_w)
    w_inc = jnp.exp(cum)
    w_inv = jnp.exp(-cum)
    w_rem = jnp.exp(tot - cum)
    w_o[...] = w_inc
    at_o[...] = (-kk * jnp.exp(cum - log_w)).astype(BF16)
    rt_o[...] = (r * w_inc).astype(BF16)
    bt_o[...] = (bv * w_inv).astype(BF16)
    kt_o[...] = (k2 * w_inv).astype(BF16)
    v_o[...] = v.astype(BF16)
    bh_o[...] = (bv * w_rem).astype(BF16)
    kh_o[...] = (k2 * w_rem).astype(BF16)


def _prep(z, convw, mur, mul, w0, a0, k_k, k_a, r_k, wdec, aup, gup, ones_bd, tri, blk):
    bsz = z.shape[0]
    tm = TM_PREP
    g3 = 3 * GROUP_W
    halo_idx = lambda i: jnp.maximum(i * (tm // 8) - 1, 0)
    row = lambda w, c: pl.BlockSpec((None, tm, w), lambda b, i: (b, i, c))
    halo = lambda w, c: pl.BlockSpec((None, 8, w), lambda b, i: (b, halo_idx(i), c))
    full = lambda shape: pl.BlockSpec(shape, lambda b, i: (0,) * len(shape))
    out_bf = jax.ShapeDtypeStruct((bsz, L_PAD, GROUP_W), BF16)
    out_f = jax.ShapeDtypeStruct((bsz, L_PAD, GROUP_W), F32)
    out_spec = pl.BlockSpec((None, tm, GROUP_W), lambda b, i: (b, i, 0))
    return pl.pallas_call(
        _prep_kernel,
        grid=(bsz, L_PAD // tm),
        in_specs=[
            row(g3, 0), row(g3, 1), row(LORA_PAD, 2 * g3 // LORA_PAD),
            halo(g3, 0), halo(g3, 1), halo(LORA_PAD, 2 * g3 // LORA_PAD),
            full((3, GROUP_W)), full((1, g3)), full((1, LORA_PAD)),
            full((1, GROUP_W)), full((1, GROUP_W)), full((1, GROUP_W)), full((1, GROUP_W)),
            full((1, GROUP_W)),
            full((LORA_PAD, GROUP_W)), full((LORA_PAD, GROUP_W)), full((LORA_PAD, GROUP_W)),
            full((GROUP_W, GROUP_W)), full((tm, tm)), full((tm, tm)),
        ],
        out_specs=[out_spec] * 11,
        out_shape=[out_bf] * 8 + [out_f] * 3,
        compiler_params=_params(("parallel", "arbitrary")),
        name="prep",
    )(z, z, z, z, z, z, convw, mur, mul, w0, a0, k_k, k_a, r_k, wdec, aup, gup, ones_bd, tri, blk)


def _block_diag(x, mask):
    return jnp.where(mask, jnp.concatenate([x] * HEADS_PER_GROUP, axis=0), 0.0).astype(BF16)


def _scan_kernel(at_ref, rt_ref, bt_ref, kt_ref, v_ref, bh_ref, kh_ref, w_ref, bonus_ref, gate_ref,
                 lnw_ref, lnb_ref, y_ref, state_ref):
    @pl.when(pl.program_id(2) == 0)
    def _():
        state_ref[...] = jnp.zeros_like(state_ref)

    f = lambda ref: ref[...].astype(F32)
    at, rt, bt, kt, v, bh, kh = map(f, (at_ref, rt_ref, bt_ref, kt_ref, v_ref, bh_ref, kh_ref))

    shape_bd = (SCAN_W, SCAN_W)
    bd_mask = (lax.broadcasted_iota(jnp.int32, shape_bd, 0) // HEAD_N
               == lax.broadcasted_iota(jnp.int32, shape_bd, 1) // HEAD_N)
    shape_c = (CHUNK, SCAN_W)
    t_idx = lax.broadcasted_iota(jnp.int32, shape_c, 0)
    s_idx = lax.broadcasted_iota(jnp.int32, shape_c, 1) % CHUNK
    bd = lambda x: _block_diag(x, bd_mask)

    ar = jnp.concatenate([at, rt], axis=0).astype(BF16)
    ab = _dot_nt(ar, bd(bt))
    ak = _dot_nt(ar, bd(kt))
    a_ab = jnp.where(s_idx < t_idx, ab[:CHUNK], 0.0)
    a_rb = jnp.where(s_idx <= t_idx, ab[CHUNK:], 0.0)
    a_ak = jnp.where(s_idx < t_idx, ak[:CHUNK], 0.0)
    a_rk = jnp.where(s_idx <= t_idx, ak[CHUNK:], 0.0)

    inv = jnp.where(s_idx == t_idx, 1.0, 0.0) + a_ab
    pw = a_ab
    for _ in range(int(math.log2(CHUNK)) - 1):
        pw = _dot(pw.astype(BF16), bd(pw))
        inv = inv + _dot(pw.astype(BF16), bd(inv))

    state = state_ref[...]
    ars = _dot_nt(ar, state.astype(BF16))
    bdv = bd(v)
    rhs = ars[:CHUNK] + _dot(a_ak.astype(BF16), bdv)
    u = _dot(inv.astype(BF16), bd(rhs))
    y = ars[CHUNK:] + _dot(a_rb.astype(BF16), bd(u)) + _dot(a_rk.astype(BF16), bdv)

    w_chunk = w_ref[CHUNK - 1:CHUNK, :]
    uv = jnp.concatenate([u, v], axis=0).astype(BF16)
    bk = jnp.concatenate([bh, kh], axis=0).astype(BF16)
    state_ref[...] = state * w_chunk + jnp.where(bd_mask, _dot_tn(uv, bk), 0.0)

    ones_bd = jnp.where(bd_mask, 1.0, 0.0).astype(BF16)
    mean = _dot(y.astype(BF16), ones_bd) * (1.0 / HEAD_N)
    yc = y - mean
    var = _dot((yc * yc).astype(BF16), ones_bd) * (1.0 / HEAD_N)
    out = yc * lax.rsqrt(var + GN_EPS) * lnw_ref[...] + lnb_ref[...] + bonus_ref[...]
    y_ref[...] = (out * gate_ref[...]).astype(BF16)


def _scan(at, rt, bt, kt, v, bh, kh, w, bonus, gate, lnw, lnb):
    bsz = at.shape[0]
    n_groups = GROUP_W // SCAN_W
    blk = pl.BlockSpec((None, CHUNK, SCAN_W), lambda b, g, c: (b, c, g))
    vec = pl.BlockSpec((1, SCAN_W), lambda b, g, c: (0, g))
    return pl.pallas_call(
        _scan_kernel,
        grid=(bsz, n_groups, N_CHUNKS),
        in_specs=[blk] * 10 + [vec, vec],
        out_specs=blk,
        out_shape=jax.ShapeDtypeStruct((bsz, L_PAD, GROUP_W), BF16),
        scratch_shapes=[pltpu.VMEM((SCAN_W, SCAN_W), F32)],
        compiler_params=_params(("parallel", "parallel", "arbitrary")),
        name="scan",
    )(at, rt, bt, kt, v, bh, kh, w, bonus, gate, lnw, lnb)


def _outproj_kernel(yc_ref, yr_ref, h_ref, w_ref, gpost_ref, gpre_ref, h1_ref, n2_ref):
    mix = _dot(yc_ref[...], w_ref[:GROUP_W, :]) + _dot(yr_ref[...], w_ref[GROUP_W:, :])
    h1 = h_ref[...] + _rms(mix, gpost_ref[...])
    h1_ref[...] = h1
    n2_ref[...] = _rms(h1, gpre_ref[...]).astype(BF16)


def _outproj(yc, yr, h2d, w, gpost, gpre):
    m = h2d.shape[0]
    tm = TM_OUT
    return pl.pallas_call(
        _outproj_kernel,
        grid=(m // tm,),
        in_specs=[
            pl.BlockSpec((tm, GROUP_W), lambda i: (i, 0)),
            pl.BlockSpec((tm, GROUP_W), lambda i: (i, 0)),
            pl.BlockSpec((tm, D_MODEL), lambda i: (i, 0)),
            pl.BlockSpec((D_MODEL, D_MODEL), lambda i: (0, 0), pipeline_mode=pl.Buffered(1)),
            pl.BlockSpec((1, D_MODEL), lambda i: (0, 0)),
            pl.BlockSpec((1, D_MODEL), lambda i: (0, 0)),
        ],
        out_specs=[pl.BlockSpec((tm, D_MODEL), lambda i: (i, 0))] * 2,
        out_shape=[jax.ShapeDtypeStruct((m, D_MODEL), F32), jax.ShapeDtypeStruct((m, D_MODEL), BF16)],
        compiler_params=_params(("parallel",)),
        name="outproj",
    )(yc, yr, h2d, w, gpost, gpre)


def _ffn_up_kernel(n2_ref, wg_ref, wu_ref, cw_ref, act_ref, carry_ref):
    t = pl.program_id(2)

    @pl.when(t == 0)
    def _():
        carry_ref[...] = jnp.zeros_like(carry_ref)

    x = n2_ref[...]
    gate = _dot(x, wg_ref[...])
    up = _dot(x, wu_ref[...])
    row = lax.broadcasted_iota(jnp.int32, (TM_UP, 1), 0)
    gate = jnp.where(jnp.logical_and(t == 0, row < PAD_FRONT), 0.0, gate)
    xs = jnp.concatenate([carry_ref[...], gate], axis=0)
    carry_ref[...] = gate[TM_UP - 8:, :]
    cw = cw_ref[...]
    gc = pltpu.roll(xs, 2, 0)[8:] * cw[0:1] + pltpu.roll(xs, 1, 0)[8:] * cw[1:2] + gate * cw[2:3]
    act_ref[...] = (gc * jax.nn.sigmoid(gc) * up).astype(BF16)


def _ffn_up(n2, w_gate_up, conv_ffn):
    bsz = n2.shape[0]
    n_tiles = D_FF // TN_UP
    return pl.pallas_call(
        _ffn_up_kernel,
        grid=(bsz, n_tiles, L_PAD // TM_UP),
        in_specs=[
            pl.BlockSpec((None, TM_UP, D_MODEL), lambda b, j, t: (b, t, 0)),
            pl.BlockSpec((D_MODEL, TN_UP), lambda b, j, t: (0, j)),
            pl.BlockSpec((D_MODEL, TN_UP), lambda b, j, t: (0, j + n_tiles)),
            pl.BlockSpec((3, TN_UP), lambda b, j, t: (0, j)),
        ],
        out_specs=pl.BlockSpec((None, TM_UP, TN_UP), lambda b, j, t: (b, t, j)),
        out_shape=jax.ShapeDtypeStruct((bsz, L_PAD, D_FF), BF16),
        scratch_shapes=[pltpu.VMEM((8, TN_UP), F32)],
        compiler_params=_params(("parallel", "parallel", "arbitrary")),
        name="ffn_up",
    )(n2, w_gate_up, w_gate_up, conv_ffn)


def _ffn_down_kernel(act_ref, w_ref, h1_ref, g_ref, out_ref, acc_ref):
    k = pl.program_id(1)

    @pl.when(k == 0)
    def _():
        acc_ref[...] = jnp.zeros_like(acc_ref)

    acc_ref[...] += _dot(act_ref[...], w_ref[...])

    @pl.when(k == pl.num_programs(1) - 1)
    def _():
        out_ref[...] = h1_ref[...] + _rms(acc_ref[...], g_ref[...])


def _ffn_down(act, w, h1, g):
    m = act.shape[0]
    tm = TM_DOWN
    return pl.pallas_call(
        _ffn_down_kernel,
        grid=(m // tm, D_FF // TK_DOWN),
        in_specs=[
            pl.BlockSpec((tm, TK_DOWN), lambda i, k: (i, k)),
            pl.BlockSpec((TK_DOWN, D_MODEL), lambda i, k: (k, 0)),
            pl.BlockSpec((tm, D_MODEL), lambda i, k: (i, 0)),
            pl.BlockSpec((1, D_MODEL), lambda i, k: (0, 0)),
        ],
        out_specs=pl.BlockSpec((tm, D_MODEL), lambda i, k: (i, 0)),
        out_shape=jax.ShapeDtypeStruct((m, D_MODEL), F32),
        scratch_shapes=[pltpu.VMEM((tm, D_MODEL), F32)],
        compiler_params=_params(("parallel", "arbitrary")),
        name="ffn_down",
    )(act, w, h1, g)


def _pad_rows(w, lo, total):
    return jnp.pad(w, ((lo, total - lo - w.shape[0]), (0, 0)))


def kernel(x, meta_tokens, g_pre_mix, w_in, conv_w, mu_rwkv, w_decay_up, w0, a_up, a0, g_up, k_k, k_a,
           r_k, ln_x_w, ln_x_b, w_out, g_post_mix, g_pre_ffn, w_gate_up, conv_ffn, w_down, g_post_ffn):
    bsz = x.shape[0]
    assert x.shape == (bsz, SEQ, D_MODEL) and meta_tokens.shape == (N_META, D_MODEL)
    row = lambda p: p.reshape(1, -1).astype(F32)

    meta = jnp.broadcast_to(meta_tokens.astype(x.dtype)[None], (bsz, N_META, D_MODEL))
    h = jnp.concatenate([jnp.zeros((bsz, PAD_FRONT, D_MODEL), x.dtype), meta, x], axis=1)
    h2d = h.reshape(bsz * L_PAD, D_MODEL)

    w_in_p = jnp.pad(w_in, ((0, 0), (0, IN_COLS_PAD - w_in.shape[1]))).astype(BF16)
    z = _inproj(h2d, row(g_pre_mix), w_in_p).reshape(bsz, L_PAD, IN_COLS_PAD)

    g3 = 3 * GROUP_W
    mu_l = jnp.pad(mu_rwkv[g3:], (0, LORA_PAD - LORA_W))
    wdec = _pad_rows(w_decay_up, 0, LORA_PAD).astype(BF16)
    aup = _pad_rows(a_up, 96, LORA_PAD).astype(BF16)
    gup = _pad_rows(g_up, 192, LORA_PAD).astype(BF16)
    head_of = jnp.arange(GROUP_W) // HEAD_N
    ones_bd = (head_of[:, None] == head_of[None, :]).astype(BF16)
    t_of = jnp.arange(TM_PREP)
    same_chunk = (t_of[:, None] // CHUNK) == (t_of[None, :] // CHUNK)
    tri = jnp.logical_and(same_chunk, t_of[None, :] <= t_of[:, None]).astype(BF16)
    blk = same_chunk.astype(BF16)
    (yconv, at, rt, bt, kt, v, bh, kh, w_inc, bonus, gate) = _prep(
        z, conv_w.astype(F32), row(mu_rwkv[:g3]), row(mu_l), row(w0), row(a0), row(k_k), row(k_a),
        row(r_k), wdec, aup, gup, ones_bd, tri, blk)

    y_rwkv = _scan(at, rt, bt, kt, v, bh, kh, w_inc, bonus, gate, row(ln_x_w), row(ln_x_b))

    m = bsz * L_PAD
    h1, n2 = _outproj(yconv.reshape(m, GROUP_W), y_rwkv.reshape(m, GROUP_W), h2d,
                      w_out.astype(BF16), row(g_post_mix), row(g_pre_ffn))

    act = _ffn_up(n2.reshape(bsz, L_PAD, D_MODEL), w_gate_up.astype(BF16), conv_ffn.astype(F32))
    out = _ffn_down(act.reshape(m, D_FF), w_down.astype(BF16), h1, row(g_post_ffn))
    return out.reshape(bsz, L_PAD, D_MODEL)[:, PAD_FRONT + N_META:]
```

```python
import functools
import math

import jax
import jax.numpy as jnp
from jax import lax
from jax.experimental import pallas as pl
from jax.experimental.pallas import tpu as pltpu

F32 = jnp.float32
BF16 = jnp.bfloat16

D_MODEL = 2048
SEQ = 8192
N_META = 16
GROUP_W = 1024
HEAD_N = 64
LORA_W = 96 + 96 + 256
LORA_PAD = 512
IN_COLS_PAD = 3 * GROUP_W + 3 * GROUP_W + LORA_PAD
D_FF = 5632
RMS_EPS = 1e-6
GN_EPS = 64e-5
DECAY_SCALE = math.exp(-0.5)

CHUNK = 64
PAD_FRONT = CHUNK - N_META
L_PAD = PAD_FRONT + N_META + SEQ
N_CHUNKS = L_PAD // CHUNK
HEADS_PER_GROUP = 4
SCAN_W = HEADS_PER_GROUP * HEAD_N

VMEM_LIMIT = 56 * 1024 * 1024

TM_PROJ = 1376
TN_PROJ = 512
TM_PREP = 192
TM_OUT = 688
TM_UP = 1376
TN_UP = 512
TM_DOWN = 688
TK_DOWN = 512


def _dot(a, b):
    return jnp.dot(a, b, preferred_element_type=F32)


def _dot_nt(a, b):
    return lax.dot_general(a, b, (((1,), (1,)), ((), ())), preferred_element_type=F32)


def _dot_tn(a, b):
    return lax.dot_general(a, b, (((0,), (0,)), ((), ())), preferred_element_type=F32)


def _rms(x, g):
    return x * lax.rsqrt(jnp.mean(x * x, axis=-1, keepdims=True) + RMS_EPS) * g


def _params(sem):
    return pltpu.CompilerParams(dimension_semantics=sem, vmem_limit_bytes=VMEM_LIMIT)


def _inproj_kernel(h_ref, g_ref, w_ref, z_ref, xn_ref):
    @pl.when(pl.program_id(1) == 0)
    def _():
        xn_ref[...] = _rms(h_ref[...], g_ref[...]).astype(BF16)

    z_ref[...] = _dot(xn_ref[...], w_ref[...])


def _inproj(h2d, g, w):
    m = h2d.shape[0]
    return pl.pallas_call(
        _inproj_kernel,
        grid=(m // TM_PROJ, IN_COLS_PAD // TN_PROJ),
        in_specs=[
            pl.BlockSpec((TM_PROJ, D_MODEL), lambda i, j: (i, 0)),
            pl.BlockSpec((1, D_MODEL), lambda i, j: (0, 0)),
            pl.BlockSpec((D_MODEL, TN_PROJ), lambda i, j: (0, j)),
        ],
        out_specs=pl.BlockSpec((TM_PROJ, TN_PROJ), lambda i, j: (i, j)),
        out_shape=jax.ShapeDtypeStruct((m, IN_COLS_PAD), F32),
        scratch_shapes=[pltpu.VMEM((TM_PROJ, D_MODEL), BF16)],
        compiler_params=_params(("parallel", "arbitrary")),
        name="inproj",
    )(h2d, g, w)


def _prev_rows(cur, halo, first, shift):
    halo = jnp.where(first, 0.0, halo)
    xs = jnp.concatenate([halo, cur], axis=0)
    return pltpu.roll(xs, shift, 0)[8:]


def _split_dot(m01, x):
    x1 = x.astype(BF16)
    r1 = x - x1.astype(F32)
    x2 = r1.astype(BF16)
    x3 = (r1 - x2.astype(F32)).astype(BF16)
    return _dot(m01, x1) + _dot(m01, x2) + _dot(m01, x3)


def _prep_kernel(zc_ref, zr_ref, zl_ref, hc_ref, hr_ref, hl_ref,
                 convw_ref, mur_ref, mul_ref, w0_ref, a0_ref, kk_ref, ka_ref, rk_ref,
                 wdec_ref, aup_ref, gup_ref, ones_ref, tri_ref, blk_ref,
                 yconv_o, at_o, rt_o, bt_o, kt_o, v_o, bh_o, kh_o, w_o, bonus_o, gate_o):
    first = pl.program_id(1) == 0
    g = GROUP_W

    zc = zc_ref[...]
    hc = hc_ref[...]
    p = zc[:, g:2 * g] * zc[:, 2 * g:]
    ph = hc[:, g:2 * g] * hc[:, 2 * g:]
    cw = convw_ref[...]
    conv = (_prev_rows(p, ph, first, 2) * cw[0:1] + _prev_rows(p, ph, first, 1) * cw[1:2]
            + p * cw[2:3])
    yconv_o[...] = (zc[:, :g] * conv).astype(BF16)

    zr = zr_ref[...]
    zs = zr + (_prev_rows(zr, hr_ref[...], first, 1) - zr) * mur_ref[...]
    zl = zl_ref[...]
    zsl = zl + (_prev_rows(zl, hl_ref[...], first, 1) - zl) * mul_ref[...]
    r = zs[:, :g]
    k = zs[:, g:2 * g]
    v = zs[:, 2 * g:]

    log_w = -DECAY_SCALE * jax.nn.sigmoid(w0_ref[...] + _dot(jnp.tanh(zsl).astype(BF16), wdec_ref[...]))
    a = jax.nn.sigmoid(a0_ref[...] + _dot(zsl.astype(BF16), aup_ref[...]))
    gate_o[...] = _dot(jax.nn.sigmoid(zsl).astype(BF16), gup_ref[...])

    ones_bd = ones_ref[...]
    kk = k * kk_ref[...]
    kk = kk / jnp.maximum(jnp.sqrt(_dot((kk * kk).astype(BF16), ones_bd)), 1e-12)
    k2 = k * (1.0 + (a - 1.0) * ka_ref[...])
    bonus_o[...] = _dot((r * k2 * rk_ref[...]).astype(BF16), ones_bd) * v
    bv = kk * a

    cum = _split_dot(tri_ref[...], log_w)
    tot = _split_dot(blk_ref[...], log_w)
    w_inc = jnp.exp(cum)
    w_inv = jnp.exp(-cum)
    w_rem = jnp.exp(tot - cum)
    w_o[...] = w_inc
    at_o[...] = (-kk * jnp.exp(cum - log_w)).astype(BF16)
    rt_o[...] = (r * w_inc).astype(BF16)
    bt_o[...] = (bv * w_inv).astype(BF16)
    kt_o[...] = (k2 * w_inv).astype(BF16)
    v_o[...] = v.astype(BF16)
    bh_o[...] = (bv * w_rem).astype(BF16)
    kh_o[...] = (k2 * w_rem).astype(BF16)


def _prep(z, convw, mur, mul, w0, a0, k_k, k_a, r_k, wdec, aup, gup, ones_bd, tri, blk):
    bsz = z.shape[0]
    tm = TM_PREP
    g3 = 3 * GROUP_W
    halo_idx = lambda i: jnp.maximum(i * (tm // 8) - 1, 0)
    row = lambda w, c: pl.BlockSpec((None, tm, w), lambda b, i: (b, i, c))
    halo = lambda w, c: pl.BlockSpec((None, 8, w), lambda b, i: (b, halo_idx(i), c))
    full = lambda shape: pl.BlockSpec(shape, lambda b, i: (0,) * len(shape))
    out_bf = jax.ShapeDtypeStruct((bsz, L_PAD, GROUP_W), BF16)
    out_f = jax.ShapeDtypeStruct((bsz, L_PAD, GROUP_W), F32)
    out_spec = pl.BlockSpec((None, tm, GROUP_W), lambda b, i: (b, i, 0))
    return pl.pallas_call(
        _prep_kernel,
        grid=(bsz, L_PAD // tm),
        in_specs=[
            row(g3, 0), row(g3, 1), row(LORA_PAD, 2 * g3 // LORA_PAD),
            halo(g3, 0), halo(g3, 1), halo(LORA_PAD, 2 * g3 // LORA_PAD),
            full((3, GROUP_W)), full((1, g3)), full((1, LORA_PAD)),
            full((1, GROUP_W)), full((1, GROUP_W)), full((1, GROUP_W)), full((1, GROUP_W)),
            full((1, GROUP_W)),
            full((LORA_PAD, GROUP_W)), full((LORA_PAD, GROUP_W)), full((LORA_PAD, GROUP_W)),
            full((GROUP_W, GROUP_W)), full((tm, tm)), full((tm, tm)),
        ],
        out_specs=[out_spec] * 11,
        out_shape=[out_bf] * 8 + [out_f] * 3,
        compiler_params=_params(("parallel", "arbitrary")),
        name="prep",
    )(z, z, z, z, z, z, convw, mur, mul, w0, a0, k_k, k_a, r_k, wdec, aup, gup, ones_bd, tri, blk)


def _block_diag(x, mask):
    return jnp.where(mask, jnp.concatenate([x] * HEADS_PER_GROUP, axis=0), 0.0).astype(BF16)


def _scan_group(lanes, at_ref, rt_ref, bt_ref, kt_ref, v_ref, bh_ref, kh_ref, w_ref, bonus_ref,
                gate_ref, lnw_ref, lnb_ref, y_ref, state_ref):
    f = lambda ref: ref[:, lanes].astype(F32)
    at, rt, bt, kt, v, bh, kh = map(f, (at_ref, rt_ref, bt_ref, kt_ref, v_ref, bh_ref, kh_ref))

    shape_bd = (SCAN_W, SCAN_W)
    bd_mask = (lax.broadcasted_iota(jnp.int32, shape_bd, 0) // HEAD_N
               == lax.broadcasted_iota(jnp.int32, shape_bd, 1) // HEAD_N)
    shape_c = (CHUNK, SCAN_W)
    t_idx = lax.broadcasted_iota(jnp.int32, shape_c, 0)
    s_idx = lax.broadcasted_iota(jnp.int32, shape_c, 1) % CHUNK
    bd = lambda x: _block_diag(x, bd_mask)
    stack = lambda a, b: jnp.concatenate([a, b], axis=0).astype(BF16)

    ar = stack(at, rt)
    ab = _dot_nt(ar, bd(bt))
    ak = _dot_nt(ar, bd(kt))
    state = state_ref[...]
    ars = _dot_nt(ar, state.astype(BF16))
    yield
    a_ab = jnp.where(s_idx < t_idx, ab[:CHUNK], 0.0)
    a_rb = jnp.where(s_idx <= t_idx, ab[CHUNK:], 0.0)
    a_ak = jnp.where(s_idx < t_idx, ak[:CHUNK], 0.0)
    a_rk = jnp.where(s_idx <= t_idx, ak[CHUNK:], 0.0)

    inv = jnp.where(s_idx == t_idx, 1.0, 0.0) + a_ab
    pw = _dot(a_ab.astype(BF16), bd(a_ab))
    akv = _dot(stack(a_ak, a_rk), bd(v))
    yield
    for _ in range(int(math.log2(CHUNK)) - 2):
        both = _dot(stack(pw, inv), bd(pw))
        yield
        pw = both[:CHUNK]
        inv = inv + both[CHUNK:]
    inv = inv + _dot(inv.astype(BF16), bd(pw))
    yield

    u = _dot(inv.astype(BF16), bd(ars[:CHUNK] + akv[:CHUNK]))
    yield
    y = ars[CHUNK:] + akv[CHUNK:] + _dot(a_rb.astype(BF16), bd(u))
    w_chunk = w_ref[CHUNK - 1:CHUNK, lanes]
    state_ref[...] = state * w_chunk + jnp.where(bd_mask, _dot_tn(stack(u, v), stack(bh, kh)), 0.0)
    yield

    ones_bd = jnp.where(bd_mask, 1.0, 0.0).astype(BF16)
    mean = _dot(y.astype(BF16), ones_bd) * (1.0 / HEAD_N)
    yield
    yc = y - mean
    var = _dot((yc * yc).astype(BF16), ones_bd) * (1.0 / HEAD_N)
    yield
    out = yc * lax.rsqrt(var + GN_EPS) * lnw_ref[:, lanes] + lnb_ref[:, lanes] + bonus_ref[:, lanes]
    y_ref[:, lanes] = (out * gate_ref[:, lanes]).astype(BF16)


def _scan_kernel(*refs):
    state_ref = refs[-1]

    @pl.when(pl.program_id(0) == 0)
    def _():
        state_ref[...] = jnp.zeros_like(state_ref)

    n_groups = GROUP_W // SCAN_W
    bsz = refs[0].shape[0]
    seq_refs, (lnw_ref, lnb_ref, y_ref) = refs[:10], refs[10:13]
    chains = [
        _scan_group(slice(g * SCAN_W, (g + 1) * SCAN_W), *[r.at[b] for r in seq_refs], lnw_ref, lnb_ref,
                    y_ref.at[b], state_ref.at[b * n_groups + g])
        for b in range(bsz) for g in range(n_groups)]
    while chains:
        chains = [c for c in chains if next(c, True) is None]


def _scan(at, rt, bt, kt, v, bh, kh, w, bonus, gate, lnw, lnb):
    bsz = at.shape[0]
    blk = pl.BlockSpec((bsz, CHUNK, GROUP_W), lambda c: (0, c, 0))
    vec = pl.BlockSpec((1, GROUP_W), lambda c: (0, 0))
    return pl.pallas_call(
        _scan_kernel,
        grid=(N_CHUNKS,),
        in_specs=[blk] * 10 + [vec, vec],
        out_specs=blk,
        out_shape=jax.ShapeDtypeStruct((bsz, L_PAD, GROUP_W), BF16),
        scratch_shapes=[pltpu.VMEM((bsz * GROUP_W // SCAN_W, SCAN_W, SCAN_W), F32)],
        compiler_params=_params(("arbitrary",)),
        name="scan",
    )(at, rt, bt, kt, v, bh, kh, w, bonus, gate, lnw, lnb)


def _outproj_kernel(yc_ref, yr_ref, h_ref, w_ref, gpost_ref, gpre_ref, h1_ref, n2_ref):
    mix = _dot(yc_ref[...], w_ref[:GROUP_W, :]) + _dot(yr_ref[...], w_ref[GROUP_W:, :])
    h1 = h_ref[...] + _rms(mix, gpost_ref[...])
    h1_ref[...] = h1
    n2_ref[...] = _rms(h1, gpre_ref[...]).astype(BF16)


def _outproj(yc, yr, h2d, w, gpost, gpre):
    m = h2d.shape[0]
    tm = TM_OUT
    return pl.pallas_call(
        _outproj_kernel,
        grid=(m // tm,),
        in_specs=[
            pl.BlockSpec((tm, GROUP_W), lambda i: (i, 0)),
            pl.BlockSpec((tm, GROUP_W), lambda i: (i, 0)),
            pl.BlockSpec((tm, D_MODEL), lambda i: (i, 0)),
            pl.BlockSpec((D_MODEL, D_MODEL), lambda i: (0, 0), pipeline_mode=pl.Buffered(1)),
            pl.BlockSpec((1, D_MODEL), lambda i: (0, 0)),
            pl.BlockSpec((1, D_MODEL), lambda i: (0, 0)),
        ],
        out_specs=[pl.BlockSpec((tm, D_MODEL), lambda i: (i, 0))] * 2,
        out_shape=[jax.ShapeDtypeStruct((m, D_MODEL), F32), jax.ShapeDtypeStruct((m, D_MODEL), BF16)],
        compiler_params=_params(("parallel",)),
        name="outproj",
    )(yc, yr, h2d, w, gpost, gpre)


def _ffn_up_kernel(n2_ref, wg_ref, wu_ref, cw_ref, act_ref, carry_ref):
    t = pl.program_id(2)

    @pl.when(t == 0)
    def _():
        carry_ref[...] = jnp.zeros_like(carry_ref)

    x = n2_ref[...]
    gate = _dot(x, wg_ref[...])
    up = _dot(x, wu_ref[...])
    row = lax.broadcasted_iota(jnp.int32, (TM_UP, 1), 0)
    gate = jnp.where(jnp.logical_and(t == 0, row < PAD_FRONT), 0.0, gate)
    xs = jnp.concatenate([carry_ref[...], gate], axis=0)
    carry_ref[...] = gate[TM_UP - 8:, :]
    cw = cw_ref[...]
    gc = pltpu.roll(xs, 2, 0)[8:] * cw[0:1] + pltpu.roll(xs, 1, 0)[8:] * cw[1:2] + gate * cw[2:3]
    act_ref[...] = (gc * jax.nn.sigmoid(gc) * up).astype(BF16)


def _ffn_up(n2, w_gate_up, conv_ffn):
    bsz = n2.shape[0]
    n_tiles = D_FF // TN_UP
    return pl.pallas_call(
        _ffn_up_kernel,
        grid=(bsz, n_tiles, L_PAD // TM_UP),
        in_specs=[
            pl.BlockSpec((None, TM_UP, D_MODEL), lambda b, j, t: (b, t, 0)),
            pl.BlockSpec((D_MODEL, TN_UP), lambda b, j, t: (0, j)),
            pl.BlockSpec((D_MODEL, TN_UP), lambda b, j, t: (0, j + n_tiles)),
            pl.BlockSpec((3, TN_UP), lambda b, j, t: (0, j)),
        ],
        out_specs=pl.BlockSpec((None, TM_UP, TN_UP), lambda b, j, t: (b, t, j)),
        out_shape=jax.ShapeDtypeStruct((bsz, L_PAD, D_FF), BF16),
        scratch_shapes=[pltpu.VMEM((8, TN_UP), F32)],
        compiler_params=_params(("parallel", "parallel", "arbitrary")),
        name="ffn_up",
    )(n2, w_gate_up, w_gate_up, conv_ffn)


def _ffn_down_kernel(act_ref, w_ref, h1_ref, g_ref, out_ref, acc_ref):
    k = pl.program_id(1)

    @pl.when(k == 0)
    def _():
        acc_ref[...] = jnp.zeros_like(acc_ref)

    acc_ref[...] += _dot(act_ref[...], w_ref[...])

    @pl.when(k == pl.num_programs(1) - 1)
    def _():
        out_ref[...] = h1_ref[...] + _rms(acc_ref[...], g_ref[...])


def _ffn_down(act, w, h1, g):
    m = act.shape[0]
    tm = TM_DOWN
    return pl.pallas_call(
        _ffn_down_kernel,
        grid=(m // tm, D_FF // TK_DOWN),
        in_specs=[
            pl.BlockSpec((tm, TK_DOWN), lambda i, k: (i, k)),
            pl.BlockSpec((TK_DOWN, D_MODEL), lambda i, k: (k, 0)),
            pl.BlockSpec((tm, D_MODEL), lambda i, k: (i, 0)),
            pl.BlockSpec((1, D_MODEL), lambda i, k: (0, 0)),
        ],
        out_specs=pl.BlockSpec((tm, D_MODEL), lambda i, k: (i, 0)),
        out_shape=jax.ShapeDtypeStruct((m, D_MODEL), F32),
        scratch_shapes=[pltpu.VMEM((tm, D_MODEL), F32)],
        compiler_params=_params(("parallel", "arbitrary")),
        name="ffn_down",
    )(act, w, h1, g)


def _pad_rows(w, lo, total):
    return jnp.pad(w, ((lo, total - lo - w.shape[0]), (0, 0)))


def kernel(x, meta_tokens, g_pre_mix, w_in, conv_w, mu_rwkv, w_decay_up, w0, a_up, a0, g_up, k_k, k_a,
           r_k, ln_x_w, ln_x_b, w_out, g_post_mix, g_pre_ffn, w_gate_up, conv_ffn, w_down, g_post_ffn):
    bsz = x.shape[0]
    assert x.shape == (bsz, SEQ, D_MODEL) and meta_tokens.shape == (N_META, D_MODEL)
    row = lambda p: p.reshape(1, -1).astype(F32)

    meta = jnp.broadcast_to(meta_tokens.astype(x.dtype)[None], (bsz, N_META, D_MODEL))
    h = jnp.concatenate([jnp.zeros((bsz, PAD_FRONT, D_MODEL), x.dtype), meta, x], axis=1)
    h2d = h.reshape(bsz * L_PAD, D_MODEL)

    w_in_p = jnp.pad(w_in, ((0, 0), (0, IN_COLS_PAD - w_in.shape[1]))).astype(BF16)
    z = _inproj(h2d, row(g_pre_mix), w_in_p).reshape(bsz, L_PAD, IN_COLS_PAD)

    g3 = 3 * GROUP_W
    mu_l = jnp.pad(mu_rwkv[g3:], (0, LORA_PAD - LORA_W))
    wdec = _pad_rows(w_decay_up, 0, LORA_PAD).astype(BF16)
    aup = _pad_rows(a_up, 96, LORA_PAD).astype(BF16)
    gup = _pad_rows(g_up, 192, LORA_PAD).astype(BF16)
    head_of = jnp.arange(GROUP_W) // HEAD_N
    ones_bd = (head_of[:, None] == head_of[None, :]).astype(BF16)
    t_of = jnp.arange(TM_PREP)
    same_chunk = (t_of[:, None] // CHUNK) == (t_of[None, :] // CHUNK)
    tri = jnp.logical_and(same_chunk, t_of[None, :] <= t_of[:, None]).astype(BF16)
    blk = same_chunk.astype(BF16)
    (yconv, at, rt, bt, kt, v, bh, kh, w_inc, bonus, gate) = _prep(
        z, conv_w.astype(F32), row(mu_rwkv[:g3]), row(mu_l), row(w0), row(a0), row(k_k), row(k_a),
        row(r_k), wdec, aup, gup, ones_bd, tri, blk)

    y_rwkv = _scan(at, rt, bt, kt, v, bh, kh, w_inc, bonus, gate, row(ln_x_w), row(ln_x_b))

    m = bsz * L_PAD
    h1, n2 = _outproj(yconv.reshape(m, GROUP_W), y_rwkv.reshape(m, GROUP_W), h2d,
                      w_out.astype(BF16), row(g_post_mix), row(g_pre_ffn))

    act = _ffn_up(n2.reshape(bsz, L_PAD, D_MODEL), w_gate_up.astype(BF16), conv_ffn.astype(F32))
    out = _ffn_down(act.reshape(m, D_FF), w_down.astype(BF16), h1, row(g_post_ffn))
    return out.reshape(bsz, L_PAD, D_MODEL)[:, PAD_FRONT + N_META:]
```

```python
import functools
import math

import jax
import jax.numpy as jnp
from jax import lax
from jax.experimental import pallas as pl
from jax.experimental.pallas import tpu as pltpu

F32 = jnp.float32
BF16 = jnp.bfloat16

D_MODEL = 2048
N_META = 16
GROUP_W = 1024
HEAD_N = 64
DECAY_LORA = 96
AAA_LORA = 96
GATE_LORA = 256
LANE = 128
LORA_A0 = LANE
LORA_G0 = 2 * LANE
LORA_PAD = LORA_G0 + GATE_LORA
IN_COLS_PAD = 3 * GROUP_W + 3 * GROUP_W + LORA_PAD
D_FF = 5632
RMS_EPS = 1e-6
GN_EPS = 64e-5
DECAY_SCALE = math.exp(-0.5)

CHUNK = 64
assert CHUNK == HEAD_N
PREFIX = CHUNK
HALO = 8
HEADS_PER_GROUP = 4
SCAN_W = HEADS_PER_GROUP * HEAD_N
N_GROUPS = GROUP_W // SCAN_W

VMEM_LIMIT = 56 * 1024 * 1024

TM_PROJ = 1024
TN_PROJ = IN_COLS_PAD // 4
TM_PREP = 256
TM_OUT = 512
TM_UP = 1024
TN_UP = 512
TM_DOWN = 512
TK_DOWN = D_FF // 4


def _dot(a, b):
    return jnp.dot(a, b, preferred_element_type=F32)


def _dot_nt(a, b):
    return lax.dot_general(a, b, (((1,), (1,)), ((), ())), preferred_element_type=F32)


def _dot_tn(a, b):
    return lax.dot_general(a, b, (((0,), (0,)), ((), ())), preferred_element_type=F32)


def _rms(x, g):
    return x * lax.rsqrt(jnp.mean(x * x, axis=-1, keepdims=True) + RMS_EPS) * g


def _params(sem):
    return pltpu.CompilerParams(dimension_semantics=sem, vmem_limit_bytes=VMEM_LIMIT)


def _inproj_kernel(h_ref, g_ref, w_ref, z_ref, xn_ref):
    @pl.when(pl.program_id(1) == 0)
    def _():
        xn_ref[...] = _rms(h_ref[...], g_ref[...]).astype(BF16)

    z_ref[...] = _dot(xn_ref[...], w_ref[...])


def _inproj(h2d, g, w, tm):
    m = h2d.shape[0]
    return pl.pallas_call(
        _inproj_kernel,
        grid=(m // tm, IN_COLS_PAD // TN_PROJ),
        in_specs=[
            pl.BlockSpec((tm, D_MODEL), lambda i, j: (i, 0)),
            pl.BlockSpec((1, D_MODEL), lambda i, j: (0, 0)),
            pl.BlockSpec((D_MODEL, TN_PROJ), lambda i, j: (0, j)),
        ],
        out_specs=pl.BlockSpec((tm, TN_PROJ), lambda i, j: (i, j)),
        out_shape=jax.ShapeDtypeStruct((m, IN_COLS_PAD), F32),
        scratch_shapes=[pltpu.VMEM((tm, D_MODEL), BF16)],
        compiler_params=_params(("parallel", "arbitrary")),
        name="inproj",
    )(h2d, g, w)


def _prev_rows(cur, halo, shift):
    xs = jnp.concatenate([halo, cur], axis=0)
    return pltpu.roll(xs, shift, 0)[HALO:]


def _split_dot(m01, x):
    x1 = x.astype(BF16)
    r1 = x - x1.astype(F32)
    x2 = r1.astype(BF16)
    x3 = (r1 - x2.astype(F32)).astype(BF16)
    return _dot(m01, x1) + _dot(m01, x2) + _dot(m01, x3)


def _prep_kernel(zc_ref, zr_ref, zl_ref, hc_ref, hr_ref, hl_ref, ic_ref, ir_ref, il_ref,
                 convw_ref, mur_ref, mul_ref, w0_ref, a0_ref, kk_ref, ka_ref, rk_ref,
                 wdec_ref, aup_ref, gup_ref, ones_ref, tri_ref, blk_ref,
                 yconv_o, at_o, rt_o, bt_o, kt_o, v_o, bh_o, kh_o, w_o, bonus_o, gate_o):
    first = pl.program_id(1) == 0
    g = GROUP_W
    hc = jnp.where(first, ic_ref[...], hc_ref[...])
    hr = jnp.where(first, ir_ref[...], hr_ref[...])
    hl = jnp.where(first, il_ref[...], hl_ref[...])

    zc = zc_ref[...]
    p = zc[:, g:2 * g] * zc[:, 2 * g:]
    ph = hc[:, g:2 * g] * hc[:, 2 * g:]
    cw = convw_ref[...]
    conv = _prev_rows(p, ph, 2) * cw[0:1] + _prev_rows(p, ph, 1) * cw[1:2] + p * cw[2:3]
    yconv_o[...] = (zc[:, :g] * conv).astype(BF16)

    zr = zr_ref[...]
    zs = zr + (_prev_rows(zr, hr, 1) - zr) * mur_ref[...]
    zl = zl_ref[...]
    zsl = zl + (_prev_rows(zl, hl, 1) - zl) * mul_ref[...]
    r = zs[:, :g]
    k = zs[:, g:2 * g]
    v = zs[:, 2 * g:]

    wd = jnp.tanh(zsl[:, :LORA_A0]).astype(BF16)
    log_w = -DECAY_SCALE * jax.nn.sigmoid(w0_ref[...] + _dot(wd, wdec_ref[...]))
    a = jax.nn.sigmoid(a0_ref[...] + _dot(zsl[:, LORA_A0:LORA_G0].astype(BF16), aup_ref[...]))
    gate_o[...] = _dot(jax.nn.sigmoid(zsl[:, LORA_G0:]).astype(BF16), gup_ref[...])

    ones_bd = ones_ref[...]
    kk = k * kk_ref[...]
    kk = kk / jnp.maximum(jnp.sqrt(_dot((kk * kk).astype(BF16), ones_bd)), 1e-12)
    k2 = k * (1.0 + (a - 1.0) * ka_ref[...])
    bonus_o[...] = _dot((r * k2 * rk_ref[...]).astype(BF16), ones_bd) * v
    bv = kk * a

    cum = _split_dot(tri_ref[...], log_w)
    tot = _split_dot(blk_ref[...], log_w)
    w_inc = jnp.exp(cum)
    w_inv = jnp.exp(-cum)
    w_rem = jnp.exp(tot - cum)
    w_o[...] = w_inc
    at_o[...] = (-kk * jnp.exp(cum - log_w)).astype(BF16)
    rt_o[...] = (r * w_inc).astype(BF16)
    bt_o[...] = (bv * w_inv).astype(BF16)
    kt_o[...] = (k2 * w_inv).astype(BF16)
    v_o[...] = v.astype(BF16)
    bh_o[...] = (bv * w_rem).astype(BF16)
    kh_o[...] = (k2 * w_rem).astype(BF16)


def _prep(z, z_init, convw, mur, mul, w0, a0, k_k, k_a, r_k, wdec, aup, gup, ones_bd, tm):
    bsz, seq = z.shape[:2]
    g3 = 3 * GROUP_W
    lora_blk = 2 * g3 // LORA_PAD
    t_of = jnp.arange(tm)
    same_chunk = (t_of[:, None] // CHUNK) == (t_of[None, :] // CHUNK)
    tri = jnp.logical_and(same_chunk, t_of[None, :] <= t_of[:, None]).astype(BF16)
    blk = same_chunk.astype(BF16)

    halo_idx = lambda i: jnp.maximum(i * (tm // HALO) - 1, 0)
    row = lambda w, c: pl.BlockSpec((None, tm, w), lambda b, i: (b, i, c))
    halo = lambda w, c: pl.BlockSpec((None, HALO, w), lambda b, i: (b, halo_idx(i), c))
    init = lambda w, c: pl.BlockSpec((HALO, w), lambda b, i: (0, c))
    full = lambda a: pl.BlockSpec(a.shape, lambda b, i: (0,) * a.ndim)
    consts = (convw, mur, mul, w0, a0, k_k, k_a, r_k, wdec, aup, gup, ones_bd, tri, blk)
    out_bf = jax.ShapeDtypeStruct((bsz, seq, GROUP_W), BF16)
    out_f = jax.ShapeDtypeStruct((bsz, seq, GROUP_W), F32)
    out_spec = pl.BlockSpec((None, tm, GROUP_W), lambda b, i: (b, i, 0))
    return pl.pallas_call(
        _prep_kernel,
        grid=(bsz, seq // tm),
        in_specs=[row(g3, 0), row(g3, 1), row(LORA_PAD, lora_blk),
                  halo(g3, 0), halo(g3, 1), halo(LORA_PAD, lora_blk),
                  init(g3, 0), init(g3, 1), init(LORA_PAD, lora_blk)] + [full(c) for c in consts],
        out_specs=[out_spec] * 11,
        out_shape=[out_bf] * 8 + [out_f] * 3,
        compiler_params=_params(("parallel", "arbitrary")),
        name="prep",
    )(z, z, z, z, z, z, z_init, z_init, z_init, *consts)


def _block_diag(x, mask):
    return jnp.where(mask, jnp.concatenate([x] * HEADS_PER_GROUP, axis=0), 0.0).astype(BF16)


def _scan_chain(lanes, at_ref, rt_ref, bt_ref, kt_ref, v_ref, bh_ref, kh_ref, w_ref, bonus_ref,
                gate_ref, lnw_ref, lnb_ref, y_ref, state_ref):
    f = lambda ref: ref[:, lanes].astype(F32)
    at, rt, bt, kt, v, bh, kh = map(f, (at_ref, rt_ref, bt_ref, kt_ref, v_ref, bh_ref, kh_ref))

    shape_bd = (SCAN_W, SCAN_W)
    bd_mask = (lax.broadcasted_iota(jnp.int32, shape_bd, 0) // HEAD_N
               == lax.broadcasted_iota(jnp.int32, shape_bd, 1) // HEAD_N)
    shape_c = (CHUNK, SCAN_W)
    t_idx = lax.broadcasted_iota(jnp.int32, shape_c, 0)
    s_idx = lax.broadcasted_iota(jnp.int32, shape_c, 1) % CHUNK
    bd = lambda x: _block_diag(x, bd_mask)
    stack = lambda a, b: jnp.concatenate([a, b], axis=0).astype(BF16)

    ar = stack(at, rt)
    ab = _dot_nt(ar, bd(bt))
    ak = _dot_nt(ar, bd(kt))
    state = state_ref[...]
    ars = _dot_nt(ar, state.astype(BF16))
    yield
    a_ab = jnp.where(s_idx < t_idx, ab[:CHUNK], 0.0)
    a_rb = jnp.where(s_idx <= t_idx, ab[CHUNK:], 0.0)
    a_ak = jnp.where(s_idx < t_idx, ak[:CHUNK], 0.0)
    a_rk = jnp.where(s_idx <= t_idx, ak[CHUNK:], 0.0)

    inv = jnp.where(s_idx == t_idx, 1.0, 0.0) + a_ab
    pw = _dot(a_ab.astype(BF16), bd(a_ab))
    akv = _dot(stack(a_ak, a_rk), bd(v))
    yield
    for _ in range(int(math.log2(CHUNK)) - 2):
        both = _dot(stack(pw, inv), bd(pw))
        yield
        pw = both[:CHUNK]
        inv = inv + both[CHUNK:]
    inv = inv + _dot(inv.astype(BF16), bd(pw))
    yield

    u = _dot(inv.astype(BF16), bd(ars[:CHUNK] + akv[:CHUNK]))
    yield
    y = ars[CHUNK:] + akv[CHUNK:] + _dot(a_rb.astype(BF16), bd(u))
    w_chunk = w_ref[CHUNK - 1:CHUNK, lanes]
    state_ref[...] = state * w_chunk + jnp.where(bd_mask, _dot_tn(stack(u, v), stack(bh, kh)), 0.0)
    yield

    ones_bd = jnp.where(bd_mask, 1.0, 0.0).astype(BF16)
    mean = _dot(y.astype(BF16), ones_bd) * (1.0 / HEAD_N)
    yield
    yc = y - mean
    var = _dot((yc * yc).astype(BF16), ones_bd) * (1.0 / HEAD_N)
    yield
    out = yc * lax.rsqrt(var + GN_EPS) * lnw_ref[:, lanes] + lnb_ref[:, lanes] + bonus_ref[:, lanes]
    y_ref[:, lanes] = (out * gate_ref[:, lanes]).astype(BF16)


def _scan_kernel(*refs):
    seq_refs, (lnw_ref, lnb_ref, state0_ref, y_ref, state_out_ref, state_ref) = refs[:10], refs[10:]

    @pl.when(pl.program_id(0) == 0)
    def _():
        state_ref[...] = state0_ref[...]

    bsz = y_ref.shape[0]
    chains = [
        _scan_chain(slice(g * SCAN_W, (g + 1) * SCAN_W), *[r.at[b] for r in seq_refs], lnw_ref, lnb_ref,
                    y_ref.at[b], state_ref.at[b * N_GROUPS + g])
        for b in range(bsz) for g in range(N_GROUPS)]
    while chains:
        chains = [c for c in chains if next(c, True) is None]

    @pl.when(pl.program_id(0) == pl.num_programs(0) - 1)
    def _():
        state_out_ref[...] = state_ref[...]


def _scan(seq_arrays, lnw, lnb, state0):
    bsz, seq = seq_arrays[0].shape[:2]
    blk = pl.BlockSpec((bsz, CHUNK, GROUP_W), lambda c: (0, c, 0))
    vec = pl.BlockSpec((1, GROUP_W), lambda c: (0, 0))
    st = pl.BlockSpec(state0.shape, lambda c: (0, 0, 0))
    return pl.pallas_call(
        _scan_kernel,
        grid=(seq // CHUNK,),
        in_specs=[blk] * 10 + [vec, vec, st],
        out_specs=[blk, st],
        out_shape=[jax.ShapeDtypeStruct((bsz, seq, GROUP_W), BF16), jax.ShapeDtypeStruct(state0.shape, F32)],
        scratch_shapes=[pltpu.VMEM(state0.shape, F32)],
        compiler_params=_params(("arbitrary",)),
        name="scan",
    )(*seq_arrays, lnw, lnb, state0)


def _outproj_kernel(yc_ref, yr_ref, h_ref, w_ref, gpost_ref, gpre_ref, h1_ref, n2_ref, wb_ref):
    @pl.when(pl.program_id(0) == 0)
    def _():
        wb_ref[...] = w_ref[...].astype(BF16)

    mix = _dot(yc_ref[...], wb_ref[:GROUP_W, :]) + _dot(yr_ref[...], wb_ref[GROUP_W:, :])
    h1 = h_ref[...] + _rms(mix, gpost_ref[...])
    h1_ref[...] = h1
    n2_ref[...] = _rms(h1, gpre_ref[...]).astype(BF16)


def _outproj(yc, yr, h2d, w, gpost, gpre, tm):
    m = h2d.shape[0]
    return pl.pallas_call(
        _outproj_kernel,
        grid=(m // tm,),
        in_specs=[
            pl.BlockSpec((tm, GROUP_W), lambda i: (i, 0)),
            pl.BlockSpec((tm, GROUP_W), lambda i: (i, 0)),
            pl.BlockSpec((tm, D_MODEL), lambda i: (i, 0)),
            pl.BlockSpec((D_MODEL, D_MODEL), lambda i: (0, 0), pipeline_mode=pl.Buffered(1)),
            pl.BlockSpec((1, D_MODEL), lambda i: (0, 0)),
            pl.BlockSpec((1, D_MODEL), lambda i: (0, 0)),
        ],
        out_specs=[pl.BlockSpec((tm, D_MODEL), lambda i: (i, 0))] * 2,
        out_shape=[jax.ShapeDtypeStruct((m, D_MODEL), F32), jax.ShapeDtypeStruct((m, D_MODEL), BF16)],
        scratch_shapes=[pltpu.VMEM((D_MODEL, D_MODEL), BF16)],
        compiler_params=_params(("arbitrary",)),
        name="outproj",
    )(yc, yr, h2d, w, gpost, gpre)


def _gate_rows_kernel(x_ref, w_ref, o_ref):
    o_ref[...] = _dot(x_ref[...], w_ref[...].astype(BF16))


def _gate_rows(x, w_gate_up):
    rows = x.shape[0]
    return pl.pallas_call(
        _gate_rows_kernel,
        grid=(D_FF // TN_UP,),
        in_specs=[pl.BlockSpec((rows, D_MODEL), lambda j: (0, 0)),
                  pl.BlockSpec((D_MODEL, TN_UP), lambda j: (0, j))],
        out_specs=pl.BlockSpec((rows, TN_UP), lambda j: (0, j)),
        out_shape=jax.ShapeDtypeStruct((rows, D_FF), F32),
        compiler_params=_params(("parallel",)),
        name="gate_rows",
    )(x, w_gate_up)


def _ffn_up_kernel(n2_ref, wg_ref, wu_ref, cw_ref, carry0_ref, act_ref, wgb_ref, wub_ref, carry_ref):
    b, t = pl.program_id(1), pl.program_id(2)

    @pl.when(jnp.logical_and(b == 0, t == 0))
    def _():
        wgb_ref[...] = wg_ref[...].astype(BF16)
        wub_ref[...] = wu_ref[...].astype(BF16)

    @pl.when(t == 0)
    def _():
        carry_ref[...] = carry0_ref[...]

    x = n2_ref[...]
    gate = _dot(x, wgb_ref[...])
    up = _dot(x, wub_ref[...])
    xs = jnp.concatenate([carry_ref[...], gate], axis=0)
    carry_ref[...] = gate[gate.shape[0] - HALO:, :]
    cw = cw_ref[...]
    gc = pltpu.roll(xs, 2, 0)[HALO:] * cw[0:1] + pltpu.roll(xs, 1, 0)[HALO:] * cw[1:2] + gate * cw[2:3]
    act_ref[...] = (gc * jax.nn.sigmoid(gc) * up).astype(BF16)


def _ffn_up(n2, w_gate_up, conv_ffn, carry0):
    bsz, seq = n2.shape[:2]
    n_tiles = D_FF // TN_UP
    return pl.pallas_call(
        _ffn_up_kernel,
        grid=(n_tiles, bsz, seq // TM_UP),
        in_specs=[
            pl.BlockSpec((None, TM_UP, D_MODEL), lambda j, b, t: (b, t, 0)),
            pl.BlockSpec((D_MODEL, TN_UP), lambda j, b, t: (0, j)),
            pl.BlockSpec((D_MODEL, TN_UP), lambda j, b, t: (0, j + n_tiles)),
            pl.BlockSpec((3, TN_UP), lambda j, b, t: (0, j)),
            pl.BlockSpec((HALO, TN_UP), lambda j, b, t: (0, j)),
        ],
        out_specs=pl.BlockSpec((None, TM_UP, TN_UP), lambda j, b, t: (b, t, j)),
        out_shape=jax.ShapeDtypeStruct((bsz, seq, D_FF), BF16),
        scratch_shapes=[pltpu.VMEM((D_MODEL, TN_UP), BF16), pltpu.VMEM((D_MODEL, TN_UP), BF16),
                        pltpu.VMEM((HALO, TN_UP), F32)],
        compiler_params=_params(("arbitrary", "arbitrary", "arbitrary")),
        name="ffn_up",
    )(n2, w_gate_up, w_gate_up, conv_ffn, carry0)


def _ffn_down_kernel(act_ref, w_ref, h1_ref, g_ref, out_ref):
    k = pl.program_id(1)
    part = _dot(act_ref[...], w_ref[...])

    @pl.when(k == 0)
    def _():
        out_ref[...] = part

    @pl.when(k > 0)
    def _():
        out_ref[...] += part

    @pl.when(k == pl.num_programs(1) - 1)
    def _():
        out_ref[...] = h1_ref[...] + _rms(out_ref[...], g_ref[...])


def _ffn_down(act, w, h1, g):
    m = act.shape[0]
    tm = TM_DOWN
    return pl.pallas_call(
        _ffn_down_kernel,
        grid=(m // tm, D_FF // TK_DOWN),
        in_specs=[
            pl.BlockSpec((tm, TK_DOWN), lambda i, k: (i, k)),
            pl.BlockSpec((TK_DOWN, D_MODEL), lambda i, k: (k, 0)),
            pl.BlockSpec((tm, D_MODEL), lambda i, k: (i, 0)),
            pl.BlockSpec((1, D_MODEL), lambda i, k: (0, 0)),
        ],
        out_specs=pl.BlockSpec((tm, D_MODEL), lambda i, k: (i, 0)),
        out_shape=jax.ShapeDtypeStruct((m, D_MODEL), F32),
        compiler_params=_params(("parallel", "arbitrary")),
        name="ffn_down",
    )(act, w, h1, g)


def kernel(x, meta_tokens, g_pre_mix, w_in, conv_w, mu_rwkv, w_decay_up, w0, a_up, a0, g_up, k_k, k_a,
           r_k, ln_x_w, ln_x_b, w_out, g_post_mix, g_pre_ffn, w_gate_up, conv_ffn, w_down, g_post_ffn):
    bsz, seq, _ = x.shape
    assert x.shape[2] == D_MODEL and meta_tokens.shape == (N_META, D_MODEL)
    row = lambda p: p.reshape(1, -1).astype(F32)
    g3 = 3 * GROUP_W

    def lora_cols(w, base):
        o_a, o_g = base + DECAY_LORA, base + DECAY_LORA + AAA_LORA
        pad = lambda n: jnp.zeros(w.shape[:-1] + (n,), w.dtype)
        return jnp.concatenate([w[..., base:o_a], pad(LORA_A0 - DECAY_LORA), w[..., o_a:o_g],
                                pad(LORA_G0 - LORA_A0 - AAA_LORA), w[..., o_g:]], axis=-1)

    w_in_p = jnp.concatenate([w_in[:, :2 * g3], lora_cols(w_in, 2 * g3)], axis=1).astype(BF16)
    mu_r, mu_l = row(mu_rwkv[:g3]), row(lora_cols(mu_rwkv, g3))
    wdec = jnp.pad(w_decay_up, ((0, LORA_A0 - DECAY_LORA), (0, 0))).astype(BF16)
    aup = jnp.pad(a_up, ((0, LORA_G0 - LORA_A0 - AAA_LORA), (0, 0))).astype(BF16)
    gup = g_up.astype(BF16)
    head_of = jnp.arange(GROUP_W) // HEAD_N
    ones_bd = (head_of[:, None] == head_of[None, :]).astype(BF16)
    prep_consts = (conv_w.astype(F32), mu_r, mu_l, row(w0), row(a0), row(k_k), row(k_a), row(r_k),
                   wdec, aup, gup, ones_bd)
    gpre, gpost, gffn = row(g_pre_mix), row(g_post_mix), row(g_pre_ffn)
    lnw, lnb = row(ln_x_w), row(ln_x_b)

    pre = jnp.concatenate([jnp.zeros((PREFIX - N_META, D_MODEL), F32), meta_tokens.astype(F32)], axis=0)
    z_pre = _inproj(pre, gpre, w_in_p, PREFIX)
    pre_out = _prep(z_pre[None], jnp.zeros((HALO, IN_COLS_PAD), F32), *prep_consts, PREFIX)
    y_pre, state_pre = _scan(pre_out[1:], lnw, lnb, jnp.zeros((N_GROUPS, SCAN_W, SCAN_W), F32))
    _, n2_pre = _outproj(pre_out[0][0], y_pre[0], pre, w_out, gpost, gffn, PREFIX)
    carry0 = _gate_rows(n2_pre[PREFIX - N_META:], w_gate_up)[N_META - HALO:]

    m = bsz * seq
    x2d = x.reshape(m, D_MODEL)
    z = _inproj(x2d, gpre, w_in_p, TM_PROJ).reshape(bsz, seq, IN_COLS_PAD)
    main_out = _prep(z, z_pre[PREFIX - HALO:], *prep_consts, TM_PREP)
    y_rwkv, _ = _scan(main_out[1:], lnw, lnb, jnp.tile(state_pre, (bsz, 1, 1)))
    h1, n2 = _outproj(main_out[0].reshape(m, GROUP_W), y_rwkv.reshape(m, GROUP_W), x2d, w_out, gpost, gffn,
                      TM_OUT)
    act = _ffn_up(n2.reshape(bsz, seq, D_MODEL), w_gate_up, conv_ffn.astype(F32), carry0)
    out = _ffn_down(act.reshape(m, D_FF), w_down.astype(BF16), h1, row(g_post_ffn))
    return out.reshape(bsz, seq, D_MODEL)
```

```python
import functools
import math

import jax
import jax.numpy as jnp
from jax import lax
from jax.experimental import pallas as pl
from jax.experimental.pallas import tpu as pltpu

F32 = jnp.float32
BF16 = jnp.bfloat16

D_MODEL = 2048
N_META = 16
GROUP_W = 1024
HEAD_N = 64
DECAY_LORA = 96
AAA_LORA = 96
GATE_LORA = 256
LANE = 128
LORA_W = DECAY_LORA + AAA_LORA + GATE_LORA
LORA_PAD = 512
WD_WIN = (0, LANE)
AD_WIN = (0, 2 * LANE)
GD_WIN = (LANE, LORA_PAD)
IN_COLS = 3 * GROUP_W + 3 * GROUP_W + LORA_W
IN_COLS_PAD = IN_COLS - LORA_W + LORA_PAD
D_FF = 5632
RMS_EPS = 1e-6
GN_EPS = 64e-5
DECAY_SCALE = math.exp(-0.5)

CHUNK = 64
assert CHUNK == HEAD_N
PREFIX = CHUNK
HALO = 8
HEADS_PER_GROUP = 4
SCAN_W = HEADS_PER_GROUP * HEAD_N
N_GROUPS = GROUP_W // SCAN_W

VMEM_LIMIT = 56 * 1024 * 1024

TM_PROJ = 1024
TN_PROJ = IN_COLS_PAD // 4
TM_PREP = 256
TM_OUT = 512
TM_UP = 1024
TN_UP = 512
TM_DOWN = 1024
TK_DOWN = D_FF // 4
EPI_ROWS = 128


def _dot(a, b):
    return jnp.dot(a, b, preferred_element_type=F32)


def _dot_nt(a, b):
    return lax.dot_general(a, b, (((1,), (1,)), ((), ())), preferred_element_type=F32)


def _dot_tn(a, b):
    return lax.dot_general(a, b, (((0,), (0,)), ((), ())), preferred_element_type=F32)


def _rms(x, g):
    return x * lax.rsqrt(jnp.mean(x * x, axis=-1, keepdims=True) + RMS_EPS) * g


def _params(sem):
    return pltpu.CompilerParams(dimension_semantics=sem, vmem_limit_bytes=VMEM_LIMIT)


def _inproj_kernel(h_ref, g_ref, w_ref, z_ref, xn_ref):
    @pl.when(pl.program_id(1) == 0)
    def _():
        xn_ref[...] = _rms(h_ref[...], g_ref[...]).astype(BF16)

    col = pl.program_id(1) * TN_PROJ + lax.broadcasted_iota(jnp.int32, (1, TN_PROJ), 1)
    z_ref[...] = jnp.where(col < IN_COLS, _dot(xn_ref[...], w_ref[...]), 0.0)


def _inproj(h2d, g, w, tm):
    m = h2d.shape[0]
    return pl.pallas_call(
        _inproj_kernel,
        grid=(m // tm, IN_COLS_PAD // TN_PROJ),
        in_specs=[
            pl.BlockSpec((tm, D_MODEL), lambda i, j: (i, 0)),
            pl.BlockSpec((1, D_MODEL), lambda i, j: (0, 0)),
            pl.BlockSpec((D_MODEL, TN_PROJ), lambda i, j: (0, j)),
        ],
        out_specs=pl.BlockSpec((tm, TN_PROJ), lambda i, j: (i, j)),
        out_shape=jax.ShapeDtypeStruct((m, IN_COLS_PAD), F32),
        scratch_shapes=[pltpu.VMEM((tm, D_MODEL), BF16)],
        compiler_params=_params(("parallel", "arbitrary")),
        name="inproj",
    )(h2d, g, w)


def _prev_rows(cur, halo, shift):
    xs = jnp.concatenate([halo, cur], axis=0)
    return pltpu.roll(xs, shift, 0)[HALO:]


def _split_dot(m01, x):
    x1 = x.astype(BF16)
    r1 = x - x1.astype(F32)
    x2 = r1.astype(BF16)
    x3 = (r1 - x2.astype(F32)).astype(BF16)
    return _dot(m01, x1) + _dot(m01, x2) + _dot(m01, x3)


def _prep_kernel(zc_ref, zr_ref, zl_ref, hc_ref, hr_ref, hl_ref, ic_ref, ir_ref, il_ref,
                 convw_ref, mur_ref, mul_ref, w0_ref, a0_ref, kk_ref, ka_ref, rk_ref,
                 wdec_ref, aup_ref, gup_ref, ones_ref, tri_ref, blk_ref,
                 yconv_o, at_o, rt_o, bt_o, kt_o, v_o, bh_o, kh_o, w_o, bonus_o, gate_o):
    first = pl.program_id(1) == 0
    g = GROUP_W
    hc = jnp.where(first, ic_ref[...], hc_ref[...])
    hr = jnp.where(first, ir_ref[...], hr_ref[...])
    hl = jnp.where(first, il_ref[...], hl_ref[...])

    zc = zc_ref[...]
    p = zc[:, g:2 * g] * zc[:, 2 * g:]
    ph = hc[:, g:2 * g] * hc[:, 2 * g:]
    cw = convw_ref[...]
    conv = _prev_rows(p, ph, 2) * cw[0:1] + _prev_rows(p, ph, 1) * cw[1:2] + p * cw[2:3]
    yconv_o[...] = (zc[:, :g] * conv).astype(BF16)

    zr = zr_ref[...]
    zs = zr + (_prev_rows(zr, hr, 1) - zr) * mur_ref[...]
    zl = zl_ref[...]
    zsl = zl + (_prev_rows(zl, hl, 1) - zl) * mul_ref[...]
    r = zs[:, :g]
    k = zs[:, g:2 * g]
    v = zs[:, 2 * g:]

    wd = jnp.tanh(zsl[:, WD_WIN[0]:WD_WIN[1]]).astype(BF16)
    log_w = -DECAY_SCALE * jax.nn.sigmoid(w0_ref[...] + _dot(wd, wdec_ref[...]))
    a = jax.nn.sigmoid(a0_ref[...] + _dot(zsl[:, AD_WIN[0]:AD_WIN[1]].astype(BF16), aup_ref[...]))
    gate_o[...] = _dot(jax.nn.sigmoid(zsl[:, GD_WIN[0]:GD_WIN[1]]).astype(BF16), gup_ref[...])

    ones_bd = ones_ref[...]
    kk = k * kk_ref[...]
    kk = kk / jnp.maximum(jnp.sqrt(_dot((kk * kk).astype(BF16), ones_bd)), 1e-12)
    k2 = k * (1.0 + (a - 1.0) * ka_ref[...])
    bonus_o[...] = _dot((r * k2 * rk_ref[...]).astype(BF16), ones_bd) * v
    bv = kk * a

    cum = _split_dot(tri_ref[...], log_w)
    tot = _split_dot(blk_ref[...], log_w)
    w_inc = jnp.exp(cum)
    w_inv = jnp.exp(-cum)
    w_rem = jnp.exp(tot - cum)
    w_o[...] = w_inc
    at_o[...] = (-kk * jnp.exp(cum - log_w)).astype(BF16)
    rt_o[...] = (r * w_inc).astype(BF16)
    bt_o[...] = (bv * w_inv).astype(BF16)
    kt_o[...] = (k2 * w_inv).astype(BF16)
    v_o[...] = v.astype(BF16)
    bh_o[...] = (bv * w_rem).astype(BF16)
    kh_o[...] = (k2 * w_rem).astype(BF16)


def _prep(z, z_init, convw, mur, mul, w0, a0, k_k, k_a, r_k, wdec, aup, gup, ones_bd, tm):
    bsz, seq = z.shape[:2]
    g3 = 3 * GROUP_W
    lora_blk = 2 * g3 // LORA_PAD
    t_of = jnp.arange(tm)
    same_chunk = (t_of[:, None] // CHUNK) == (t_of[None, :] // CHUNK)
    tri = jnp.logical_and(same_chunk, t_of[None, :] <= t_of[:, None]).astype(BF16)
    blk = same_chunk.astype(BF16)

    halo_idx = lambda i: jnp.maximum(i * (tm // HALO) - 1, 0)
    row = lambda w, c: pl.BlockSpec((None, tm, w), lambda b, i: (b, i, c))
    halo = lambda w, c: pl.BlockSpec((None, HALO, w), lambda b, i: (b, halo_idx(i), c))
    init = lambda w, c: pl.BlockSpec((HALO, w), lambda b, i: (0, c))
    full = lambda a: pl.BlockSpec(a.shape, lambda b, i: (0,) * a.ndim)
    consts = (convw, mur, mul, w0, a0, k_k, k_a, r_k, wdec, aup, gup, ones_bd, tri, blk)
    out_bf = jax.ShapeDtypeStruct((bsz, seq, GROUP_W), BF16)
    out_f = jax.ShapeDtypeStruct((bsz, seq, GROUP_W), F32)
    out_spec = pl.BlockSpec((None, tm, GROUP_W), lambda b, i: (b, i, 0))
    return pl.pallas_call(
        _prep_kernel,
        grid=(bsz, seq // tm),
        in_specs=[row(g3, 0), row(g3, 1), row(LORA_PAD, lora_blk),
                  halo(g3, 0), halo(g3, 1), halo(LORA_PAD, lora_blk),
                  init(g3, 0), init(g3, 1), init(LORA_PAD, lora_blk)] + [full(c) for c in consts],
        out_specs=[out_spec] * 11,
        out_shape=[out_bf] * 8 + [out_f] * 3,
        compiler_params=_params(("parallel", "arbitrary")),
        name="prep",
    )(z, z, z, z, z, z, z_init, z_init, z_init, *consts)


def _block_diag(x, mask):
    return jnp.where(mask, jnp.concatenate([x] * HEADS_PER_GROUP, axis=0), 0.0).astype(BF16)


def _scan_chain(lanes, at_ref, rt_ref, bt_ref, kt_ref, v_ref, bh_ref, kh_ref, w_ref, bonus_ref,
                gate_ref, lnw_ref, lnb_ref, y_ref, state_ref):
    f = lambda ref: ref[:, lanes].astype(F32)
    at, rt, bt, kt, v, bh, kh = map(f, (at_ref, rt_ref, bt_ref, kt_ref, v_ref, bh_ref, kh_ref))

    shape_bd = (SCAN_W, SCAN_W)
    bd_mask = (lax.broadcasted_iota(jnp.int32, shape_bd, 0) // HEAD_N
               == lax.broadcasted_iota(jnp.int32, shape_bd, 1) // HEAD_N)
    shape_c = (CHUNK, SCAN_W)
    t_idx = lax.broadcasted_iota(jnp.int32, shape_c, 0)
    s_idx = lax.broadcasted_iota(jnp.int32, shape_c, 1) % CHUNK
    bd = lambda x: _block_diag(x, bd_mask)
    stack = lambda a, b: jnp.concatenate([a, b], axis=0).astype(BF16)

    ar = stack(at, rt)
    ab = _dot_nt(ar, bd(bt))
    ak = _dot_nt(ar, bd(kt))
    state = state_ref[...]
    ars = _dot_nt(ar, state.astype(BF16))
    yield
    a_ab = jnp.where(s_idx < t_idx, ab[:CHUNK], 0.0)
    a_rb = jnp.where(s_idx <= t_idx, ab[CHUNK:], 0.0)
    a_ak = jnp.where(s_idx < t_idx, ak[:CHUNK], 0.0)
    a_rk = jnp.where(s_idx <= t_idx, ak[CHUNK:], 0.0)

    inv = jnp.where(s_idx == t_idx, 1.0, 0.0) + a_ab
    pw = _dot(a_ab.astype(BF16), bd(a_ab))
    akv = _dot(stack(a_ak, a_rk), bd(v))
    yield
    for _ in range(int(math.log2(CHUNK)) - 2):
        both = _dot(stack(pw, inv), bd(pw))
        yield
        pw = both[:CHUNK]
        inv = inv + both[CHUNK:]
    inv = inv + _dot(inv.astype(BF16), bd(pw))
    yield

    u = _dot(inv.astype(BF16), bd(ars[:CHUNK] + akv[:CHUNK]))
    yield
    y = ars[CHUNK:] + akv[CHUNK:] + _dot(a_rb.astype(BF16), bd(u))
    w_chunk = w_ref[CHUNK - 1:CHUNK, lanes]
    state_ref[...] = state * w_chunk + jnp.where(bd_mask, _dot_tn(stack(u, v), stack(bh, kh)), 0.0)
    yield

    ones_bd = jnp.where(bd_mask, 1.0, 0.0).astype(BF16)
    mean = _dot(y.astype(BF16), ones_bd) * (1.0 / HEAD_N)
    yield
    yc = y - mean
    var = _dot((yc * yc).astype(BF16), ones_bd) * (1.0 / HEAD_N)
    yield
    out = yc * lax.rsqrt(var + GN_EPS) * lnw_ref[:, lanes] + lnb_ref[:, lanes] + bonus_ref[:, lanes]
    y_ref[:, lanes] = (out * gate_ref[:, lanes]).astype(BF16)


def _scan_kernel(*refs):
    seq_refs, (lnw_ref, lnb_ref, state0_ref, y_ref, state_out_ref, state_ref) = refs[:10], refs[10:]

    @pl.when(pl.program_id(0) == 0)
    def _():
        state_ref[...] = state0_ref[...]

    bsz = y_ref.shape[0]
    chains = [
        _scan_chain(slice(g * SCAN_W, (g + 1) * SCAN_W), *[r.at[b] for r in seq_refs], lnw_ref, lnb_ref,
                    y_ref.at[b], state_ref.at[b * N_GROUPS + g])
        for b in range(bsz) for g in range(N_GROUPS)]
    while chains:
        chains = [c for c in chains if next(c, True) is None]

    @pl.when(pl.program_id(0) == pl.num_programs(0) - 1)
    def _():
        state_out_ref[...] = state_ref[...]


def _scan(seq_arrays, lnw, lnb, state0):
    bsz, seq = seq_arrays[0].shape[:2]
    blk = pl.BlockSpec((bsz, CHUNK, GROUP_W), lambda c: (0, c, 0))
    vec = pl.BlockSpec((1, GROUP_W), lambda c: (0, 0))
    st = pl.BlockSpec(state0.shape, lambda c: (0, 0, 0))
    return pl.pallas_call(
        _scan_kernel,
        grid=(seq // CHUNK,),
        in_specs=[blk] * 10 + [vec, vec, st],
        out_specs=[blk, st],
        out_shape=[jax.ShapeDtypeStruct((bsz, seq, GROUP_W), BF16), jax.ShapeDtypeStruct(state0.shape, F32)],
        scratch_shapes=[pltpu.VMEM(state0.shape, F32)],
        compiler_params=_params(("arbitrary",)),
        name="scan",
    )(*seq_arrays, lnw, lnb, state0)


def _outproj_kernel(yc_ref, yr_ref, h_ref, w_ref, gpost_ref, gpre_ref, h1_ref, n2_ref, wb_ref):
    @pl.when(pl.program_id(0) == 0)
    def _():
        wb_ref[...] = w_ref[...].astype(BF16)

    mix = _dot(yc_ref[...], wb_ref[:GROUP_W, :]) + _dot(yr_ref[...], wb_ref[GROUP_W:, :])
    h1 = h_ref[...] + _rms(mix, gpost_ref[...])
    h1_ref[...] = h1
    n2_ref[...] = _rms(h1, gpre_ref[...]).astype(BF16)


def _outproj(yc, yr, h2d, w, gpost, gpre, tm):
    m = h2d.shape[0]
    return pl.pallas_call(
        _outproj_kernel,
        grid=(m // tm,),
        in_specs=[
            pl.BlockSpec((tm, GROUP_W), lambda i: (i, 0)),
            pl.BlockSpec((tm, GROUP_W), lambda i: (i, 0)),
            pl.BlockSpec((tm, D_MODEL), lambda i: (i, 0)),
            pl.BlockSpec((D_MODEL, D_MODEL), lambda i: (0, 0), pipeline_mode=pl.Buffered(1)),
            pl.BlockSpec((1, D_MODEL), lambda i: (0, 0)),
            pl.BlockSpec((1, D_MODEL), lambda i: (0, 0)),
        ],
        out_specs=[pl.BlockSpec((tm, D_MODEL), lambda i: (i, 0))] * 2,
        out_shape=[jax.ShapeDtypeStruct((m, D_MODEL), F32), jax.ShapeDtypeStruct((m, D_MODEL), BF16)],
        scratch_shapes=[pltpu.VMEM((D_MODEL, D_MODEL), BF16)],
        compiler_params=_params(("arbitrary",)),
        name="outproj",
    )(yc, yr, h2d, w, gpost, gpre)


def _gate_rows_kernel(x_ref, w_ref, o_ref):
    o_ref[...] = _dot(x_ref[...], w_ref[...].astype(BF16))


def _gate_rows(x, w_gate_up):
    rows = x.shape[0]
    return pl.pallas_call(
        _gate_rows_kernel,
        grid=(D_FF // TN_UP,),
        in_specs=[pl.BlockSpec((rows, D_MODEL), lambda j: (0, 0)),
                  pl.BlockSpec((D_MODEL, TN_UP), lambda j: (0, j))],
        out_specs=pl.BlockSpec((rows, TN_UP), lambda j: (0, j)),
        out_shape=jax.ShapeDtypeStruct((rows, D_FF), F32),
        compiler_params=_params(("parallel",)),
        name="gate_rows",
    )(x, w_gate_up)


def _ffn_up_kernel(n2_ref, wg_ref, wu_ref, cw_ref, carry0_ref, wd_ref, act_ref, wdb_ref, wgb_ref, wub_ref,
                   carry_ref):
    b, t = pl.program_id(1), pl.program_id(2)

    @pl.when(jnp.logical_and(b == 0, t == 0))
    def _():
        wgb_ref[...] = wg_ref[...].astype(BF16)
        wub_ref[...] = wu_ref[...].astype(BF16)
        wdb_ref[...] = wd_ref[...].astype(BF16)

    @pl.when(t == 0)
    def _():
        carry_ref[...] = carry0_ref[...]

    x = n2_ref[...]
    gate = _dot(x, wgb_ref[...])
    up = _dot(x, wub_ref[...])
    xs = jnp.concatenate([carry_ref[...], gate], axis=0)
    carry_ref[...] = gate[gate.shape[0] - HALO:, :]
    cw = cw_ref[...]
    gc = pltpu.roll(xs, 2, 0)[HALO:] * cw[0:1] + pltpu.roll(xs, 1, 0)[HALO:] * cw[1:2] + gate * cw[2:3]
    act_ref[...] = (gc * jax.nn.sigmoid(gc) * up).astype(BF16)


def _ffn_up(n2, w_gate_up, conv_ffn, carry0, w_down):
    bsz, seq = n2.shape[:2]
    n_tiles = D_FF // TN_UP
    return pl.pallas_call(
        _ffn_up_kernel,
        grid=(n_tiles, bsz, seq // TM_UP),
        in_specs=[
            pl.BlockSpec((None, TM_UP, D_MODEL), lambda j, b, t: (b, t, 0)),
            pl.BlockSpec((D_MODEL, TN_UP), lambda j, b, t: (0, j)),
            pl.BlockSpec((D_MODEL, TN_UP), lambda j, b, t: (0, j + n_tiles)),
            pl.BlockSpec((3, TN_UP), lambda j, b, t: (0, j)),
            pl.BlockSpec((HALO, TN_UP), lambda j, b, t: (0, j)),
            pl.BlockSpec((TN_UP, D_MODEL), lambda j, b, t: (j, 0)),
        ],
        out_specs=[pl.BlockSpec((None, TM_UP, TN_UP), lambda j, b, t: (b, t, j)),
                   pl.BlockSpec((TN_UP, D_MODEL), lambda j, b, t: (j, 0))],
        out_shape=[jax.ShapeDtypeStruct((bsz, seq, D_FF), BF16), jax.ShapeDtypeStruct((D_FF, D_MODEL), BF16)],
        scratch_shapes=[pltpu.VMEM((D_MODEL, TN_UP), BF16), pltpu.VMEM((D_MODEL, TN_UP), BF16),
                        pltpu.VMEM((HALO, TN_UP), F32)],
        compiler_params=_params(("arbitrary", "arbitrary", "arbitrary")),
        name="ffn_up",
    )(n2, w_gate_up, w_gate_up, conv_ffn, carry0, w_down)


def _ffn_down_kernel(act_ref, w_ref, h1_ref, g_ref, out_ref):
    k = pl.program_id(1)

    @pl.when(k == 0)
    def _():
        out_ref[...] = _dot(act_ref[...], w_ref[...])

    @pl.when(k > 0)
    def _():
        out_ref[...] += _dot(act_ref[...], w_ref[...])

    @pl.when(k == pl.num_programs(1) - 1)
    def _():
        def body(c, carry):
            rows = pl.ds(pl.multiple_of(c * EPI_ROWS, EPI_ROWS), EPI_ROWS)
            out_ref[rows, :] = h1_ref[rows, :] + _rms(out_ref[rows, :], g_ref[...])
            return carry

        lax.fori_loop(0, out_ref.shape[0] // EPI_ROWS, body, 0)


def _ffn_down(act, w, h1, g):
    m = act.shape[0]
    tm = TM_DOWN
    return pl.pallas_call(
        _ffn_down_kernel,
        grid=(m // tm, D_FF // TK_DOWN),
        in_specs=[
            pl.BlockSpec((tm, TK_DOWN), lambda i, k: (i, k)),
            pl.BlockSpec((TK_DOWN, D_MODEL), lambda i, k: (k, 0)),
            pl.BlockSpec((tm, D_MODEL), lambda i, k: (i, 0)),
            pl.BlockSpec((1, D_MODEL), lambda i, k: (0, 0)),
        ],
        out_specs=pl.BlockSpec((tm, D_MODEL), lambda i, k: (i, 0)),
        out_shape=jax.ShapeDtypeStruct((m, D_MODEL), F32),
        compiler_params=_params(("parallel", "arbitrary")),
        name="ffn_down",
    )(act, w, h1, g)


def kernel(x, meta_tokens, g_pre_mix, w_in, conv_w, mu_rwkv, w_decay_up, w0, a_up, a0, g_up, k_k, k_a,
           r_k, ln_x_w, ln_x_b, w_out, g_post_mix, g_pre_ffn, w_gate_up, conv_ffn, w_down, g_post_ffn):
    bsz, seq, _ = x.shape
    assert x.shape[2] == D_MODEL and meta_tokens.shape == (N_META, D_MODEL)
    row = lambda p: p.reshape(1, -1).astype(F32)
    g3 = 3 * GROUP_W

    def window_rows(w, start, win):
        return jnp.pad(w, ((start - win[0], win[1] - start - w.shape[0]), (0, 0))).astype(BF16)

    w_in_p = w_in.astype(BF16)
    mu_r, mu_l = row(mu_rwkv[:g3]), row(jnp.pad(mu_rwkv[g3:], (0, LORA_PAD - LORA_W)))
    wdec = window_rows(w_decay_up, 0, WD_WIN)
    aup = window_rows(a_up, DECAY_LORA, AD_WIN)
    gup = window_rows(g_up, DECAY_LORA + AAA_LORA, GD_WIN)
    head_of = jnp.arange(GROUP_W) // HEAD_N
    ones_bd = (head_of[:, None] == head_of[None, :]).astype(BF16)
    prep_consts = (conv_w.astype(F32), mu_r, mu_l, row(w0), row(a0), row(k_k), row(k_a), row(r_k),
                   wdec, aup, gup, ones_bd)
    gpre, gpost, gffn = row(g_pre_mix), row(g_post_mix), row(g_pre_ffn)
    lnw, lnb = row(ln_x_w), row(ln_x_b)

    pre = jnp.concatenate([jnp.zeros((PREFIX - N_META, D_MODEL), F32), meta_tokens.astype(F32)], axis=0)
    z_pre = _inproj(pre, gpre, w_in_p, PREFIX)
    pre_out = _prep(z_pre[None], jnp.zeros((HALO, IN_COLS_PAD), F32), *prep_consts, PREFIX)
    y_pre, state_pre = _scan(pre_out[1:], lnw, lnb, jnp.zeros((N_GROUPS, SCAN_W, SCAN_W), F32))
    _, n2_pre = _outproj(pre_out[0][0], y_pre[0], pre, w_out, gpost, gffn, PREFIX)
    carry0 = _gate_rows(n2_pre[PREFIX - N_META:], w_gate_up)[N_META - HALO:]

    m = bsz * seq
    x2d = x.reshape(m, D_MODEL)
    z = _inproj(x2d, gpre, w_in_p, TM_PROJ).reshape(bsz, seq, IN_COLS_PAD)
    main_out = _prep(z, z_pre[PREFIX - HALO:], *prep_consts, TM_PREP)
    y_rwkv, _ = _scan(main_out[1:], lnw, lnb, jnp.tile(state_pre, (bsz, 1, 1)))
    h1, n2 = _outproj(main_out[0].reshape(m, GROUP_W), y_rwkv.reshape(m, GROUP_W), x2d, w_out, gpost, gffn,
                      TM_OUT)
    act, w_down_b = _ffn_up(n2.reshape(bsz, seq, D_MODEL), w_gate_up, conv_ffn.astype(F32), carry0, w_down)
    out = _ffn_down(act.reshape(m, D_FF), w_down_b, h1, row(g_post_ffn))
    return out.reshape(bsz, seq, D_MODEL)
```

```python
import functools
import math

import jax
import jax.numpy as jnp
from jax import lax
from jax.experimental import pallas as pl
from jax.experimental.pallas import tpu as pltpu

F32 = jnp.float32
BF16 = jnp.bfloat16

D_MODEL = 2048
N_META = 16
GROUP_W = 1024
HEAD_N = 64
DECAY_LORA = 96
AAA_LORA = 96
GATE_LORA = 256
LANE = 128
LORA_W = DECAY_LORA + AAA_LORA + GATE_LORA
LORA_PAD = 512
WD_WIN = (0, LANE)
AD_WIN = (0, 2 * LANE)
GD_WIN = (LANE, LORA_PAD)
IN_COLS = 3 * GROUP_W + 3 * GROUP_W + LORA_W
IN_COLS_PAD = IN_COLS - LORA_W + LORA_PAD
D_FF = 5632
RMS_EPS = 1e-6
GN_EPS = 64e-5
DECAY_SCALE = math.exp(-0.5)

CHUNK = 64
assert CHUNK == HEAD_N
PREFIX = CHUNK
HALO = 8
HEADS_PER_GROUP = 4
SCAN_W = HEADS_PER_GROUP * HEAD_N
N_GROUPS = GROUP_W // SCAN_W
SCAN_CHUNKS_PER_STEP = 4
SCAN_STAGGER = 2

VMEM_LIMIT = 56 * 1024 * 1024

TM_PROJ = 1024
TN_PROJ = IN_COLS_PAD // 4
TN_PROJ_PRE = 512
TM_PREP = 256
TM_OUT = 512
TM_UP = 1024
TN_UP = 512
TM_DOWN = 1024
TK_DOWN = D_FF // 4
EPI_ROWS = 128


def _dot(a, b):
    return jnp.dot(a, b, preferred_element_type=F32)


def _dot_nt(a, b):
    return lax.dot_general(a, b, (((1,), (1,)), ((), ())), preferred_element_type=F32)


def _dot_tn(a, b):
    return lax.dot_general(a, b, (((0,), (0,)), ((), ())), preferred_element_type=F32)


def _rms(x, g):
    return x * lax.rsqrt(jnp.mean(x * x, axis=-1, keepdims=True) + RMS_EPS) * g


def _params(sem):
    return pltpu.CompilerParams(dimension_semantics=sem, vmem_limit_bytes=VMEM_LIMIT)


def _inproj_kernel(h_ref, g_ref, w_ref, z_ref, *rest, cast):
    xn_ref = rest[-1]

    @pl.when(pl.program_id(1) == 0)
    def _():
        xn_ref[...] = _rms(h_ref[...], g_ref[...]).astype(BF16)

    if cast:
        tn = w_ref.shape[1]
        col = pl.program_id(1) * tn + lax.broadcasted_iota(jnp.int32, (1, tn), 1)
        rest[0][...] = jnp.where(col < IN_COLS, w_ref[...], 0.0).astype(BF16)
        z_ref[...] = _dot(xn_ref[...], rest[0][...])
    else:
        z_ref[...] = _dot(xn_ref[...], w_ref[...])


def _inproj(h2d, g, w, tm, tn, cast):
    m = h2d.shape[0]
    z_spec = pl.BlockSpec((tm, tn), lambda i, j: (i, j))
    z_shape = jax.ShapeDtypeStruct((m, IN_COLS_PAD), F32)
    w_spec = pl.BlockSpec((D_MODEL, tn), lambda i, j: (0, j))
    return pl.pallas_call(
        functools.partial(_inproj_kernel, cast=cast),
        grid=(m // tm, IN_COLS_PAD // tn),
        in_specs=[
            pl.BlockSpec((tm, D_MODEL), lambda i, j: (i, 0)),
            pl.BlockSpec((1, D_MODEL), lambda i, j: (0, 0)),
            w_spec,
        ],
        out_specs=[z_spec, w_spec] if cast else z_spec,
        out_shape=[z_shape, jax.ShapeDtypeStruct((D_MODEL, IN_COLS_PAD), BF16)] if cast else z_shape,
        scratch_shapes=[pltpu.VMEM((tm, D_MODEL), BF16)],
        compiler_params=_params(("parallel", "arbitrary")),
        name="inproj",
    )(h2d, g, w)


def _prev_rows(cur, halo, shift):
    xs = jnp.concatenate([halo, cur], axis=0)
    return pltpu.roll(xs, shift, 0)[HALO:]


def _split_dot(m01, x):
    x1 = x.astype(BF16)
    r1 = x - x1.astype(F32)
    x2 = r1.astype(BF16)
    x3 = (r1 - x2.astype(F32)).astype(BF16)
    return _dot(m01, x1) + _dot(m01, x2) + _dot(m01, x3)


def _prep_kernel(zc_ref, zr_ref, zl_ref, hc_ref, hr_ref, hl_ref, ic_ref, ir_ref, il_ref,
                 convw_ref, mur_ref, mul_ref, w0_ref, a0_ref, kk_ref, ka_ref, rk_ref,
                 wdec_ref, aup_ref, gup_ref, ones_ref, tri_ref, blk_ref,
                 yconv_o, at_o, rt_o, bt_o, kt_o, v_o, bh_o, kh_o, w_o, bonus_o, gate_o):
    first = pl.program_id(1) == 0
    g = GROUP_W
    hc = jnp.where(first, ic_ref[...], hc_ref[...])
    hr = jnp.where(first, ir_ref[...], hr_ref[...])
    hl = jnp.where(first, il_ref[...], hl_ref[...])

    zc = zc_ref[...]
    p = zc[:, g:2 * g] * zc[:, 2 * g:]
    ph = hc[:, g:2 * g] * hc[:, 2 * g:]
    cw = convw_ref[...]
    conv = _prev_rows(p, ph, 2) * cw[0:1] + _prev_rows(p, ph, 1) * cw[1:2] + p * cw[2:3]
    yconv_o[...] = (zc[:, :g] * conv).astype(BF16)

    zr = zr_ref[...]
    zs = zr + (_prev_rows(zr, hr, 1) - zr) * mur_ref[...]
    zl = zl_ref[...]
    zsl = zl + (_prev_rows(zl, hl, 1) - zl) * mul_ref[...]
    r = zs[:, :g]
    k = zs[:, g:2 * g]
    v = zs[:, 2 * g:]

    wd = jnp.tanh(zsl[:, WD_WIN[0]:WD_WIN[1]]).astype(BF16)
    log_w = -DECAY_SCALE * jax.nn.sigmoid(w0_ref[...] + _dot(wd, wdec_ref[...]))
    a = jax.nn.sigmoid(a0_ref[...] + _dot(zsl[:, AD_WIN[0]:AD_WIN[1]].astype(BF16), aup_ref[...]))
    gate_o[...] = _dot(jax.nn.sigmoid(zsl[:, GD_WIN[0]:GD_WIN[1]]).astype(BF16), gup_ref[...])

    ones_bd = ones_ref[...]
    kk = k * kk_ref[...]
    kk = kk / jnp.maximum(jnp.sqrt(_dot((kk * kk).astype(BF16), ones_bd)), 1e-12)
    k2 = k * (1.0 + (a - 1.0) * ka_ref[...])
    bonus_o[...] = _dot((r * k2 * rk_ref[...]).astype(BF16), ones_bd) * v
    bv = kk * a

    cum = _split_dot(tri_ref[...], log_w)
    tot = _split_dot(blk_ref[...], log_w)
    w_inc = jnp.exp(cum)
    w_inv = jnp.exp(-cum)
    w_rem = jnp.exp(tot - cum)
    w_o[...] = w_inc
    at_o[...] = (-kk * jnp.exp(cum - log_w)).astype(BF16)
    rt_o[...] = (r * w_inc).astype(BF16)
    bt_o[...] = (bv * w_inv).astype(BF16)
    kt_o[...] = (k2 * w_inv).astype(BF16)
    v_o[...] = v.astype(BF16)
    bh_o[...] = (bv * w_rem).astype(BF16)
    kh_o[...] = (k2 * w_rem).astype(BF16)


def _prep(z, z_init, convw, mur, mul, w0, a0, k_k, k_a, r_k, wdec, aup, gup, ones_bd, tm):
    bsz, seq = z.shape[:2]
    g3 = 3 * GROUP_W
    lora_blk = 2 * g3 // LORA_PAD
    t_of = jnp.arange(tm)
    same_chunk = (t_of[:, None] // CHUNK) == (t_of[None, :] // CHUNK)
    tri = jnp.logical_and(same_chunk, t_of[None, :] <= t_of[:, None]).astype(BF16)
    blk = same_chunk.astype(BF16)

    halo_idx = lambda i: jnp.maximum(i * (tm // HALO) - 1, 0)
    row = lambda w, c: pl.BlockSpec((None, tm, w), lambda b, i: (b, i, c))
    halo = lambda w, c: pl.BlockSpec((None, HALO, w), lambda b, i: (b, halo_idx(i), c))
    init = lambda w, c: pl.BlockSpec((HALO, w), lambda b, i: (0, c))
    full = lambda a: pl.BlockSpec(a.shape, lambda b, i: (0,) * a.ndim)
    consts = (convw, mur, mul, w0, a0, k_k, k_a, r_k, wdec, aup, gup, ones_bd, tri, blk)
    out_bf = jax.ShapeDtypeStruct((bsz, seq, GROUP_W), BF16)
    out_f = jax.ShapeDtypeStruct((bsz, seq, GROUP_W), F32)
    out_spec = pl.BlockSpec((None, tm, GROUP_W), lambda b, i: (b, i, 0))
    return pl.pallas_call(
        _prep_kernel,
        grid=(bsz, seq // tm),
        in_specs=[row(g3, 0), row(g3, 1), row(LORA_PAD, lora_blk),
                  halo(g3, 0), halo(g3, 1), halo(LORA_PAD, lora_blk),
                  init(g3, 0), init(g3, 1), init(LORA_PAD, lora_blk)] + [full(c) for c in consts],
        out_specs=[out_spec] * 11,
        out_shape=[out_bf] * 8 + [out_f] * 3,
        compiler_params=_params(("parallel", "arbitrary")),
        name="prep",
    )(z, z, z, z, z, z, z_init, z_init, z_init, *consts)


def _block_diag(x, mask):
    return jnp.where(mask, jnp.concatenate([x] * HEADS_PER_GROUP, axis=0), 0.0).astype(BF16)


def _scan_chain(chain_id, chunk, ready, rows, lanes, at_ref, rt_ref, bt_ref, kt_ref, v_ref, bh_ref, kh_ref, w_ref,
                bonus_ref, gate_ref, lnw_ref, lnb_ref, y_ref, state_ref):
    f = lambda ref: ref[rows, lanes].astype(F32)
    at, rt, bt, kt, v, bh, kh = map(f, (at_ref, rt_ref, bt_ref, kt_ref, v_ref, bh_ref, kh_ref))

    shape_bd = (SCAN_W, SCAN_W)
    bd_mask = (lax.broadcasted_iota(jnp.int32, shape_bd, 0) // HEAD_N
               == lax.broadcasted_iota(jnp.int32, shape_bd, 1) // HEAD_N)
    shape_c = (CHUNK, SCAN_W)
    t_idx = lax.broadcasted_iota(jnp.int32, shape_c, 0)
    s_idx = lax.broadcasted_iota(jnp.int32, shape_c, 1) % CHUNK
    bd = lambda x: _block_diag(x, bd_mask)
    stack = lambda a, b: jnp.concatenate([a, b], axis=0).astype(BF16)

    ar = stack(at, rt)
    ab = _dot_nt(ar, bd(bt))
    ak = _dot_nt(ar, bd(kt))
    yield
    a_ab = jnp.where(s_idx < t_idx, ab[:CHUNK], 0.0)
    a_rb = jnp.where(s_idx <= t_idx, ab[CHUNK:], 0.0)
    a_ak = jnp.where(s_idx < t_idx, ak[:CHUNK], 0.0)
    a_rk = jnp.where(s_idx <= t_idx, ak[CHUNK:], 0.0)

    inv = jnp.where(s_idx == t_idx, 1.0, 0.0) + a_ab
    pw = _dot(a_ab.astype(BF16), bd(a_ab))
    akv = _dot(stack(a_ak, a_rk), bd(v))
    yield
    for _ in range(int(math.log2(CHUNK)) - 2):
        both = _dot(stack(pw, inv), bd(pw))
        yield
        pw = both[:CHUNK]
        inv = inv + both[CHUNK:]
    inv = inv + _dot(inv.astype(BF16), bd(pw))
    assert chunk == 0 or (chain_id, chunk - 1) in ready
    state = state_ref[...]
    ars = _dot_nt(ar, state.astype(BF16))
    yield

    u = _dot(inv.astype(BF16), bd(ars[:CHUNK] + akv[:CHUNK]))
    yield
    y = ars[CHUNK:] + akv[CHUNK:] + _dot(a_rb.astype(BF16), bd(u))
    w_chunk = w_ref[rows.stop - 1:rows.stop, lanes]
    state_ref[...] = state * w_chunk + jnp.where(bd_mask, _dot_tn(stack(u, v), stack(bh, kh)), 0.0)
    ready.add((chain_id, chunk))
    yield

    ones_bd = jnp.where(bd_mask, 1.0, 0.0).astype(BF16)
    mean = _dot(y.astype(BF16), ones_bd) * (1.0 / HEAD_N)
    yield
    yc = y - mean
    var = _dot((yc * yc).astype(BF16), ones_bd) * (1.0 / HEAD_N)
    yield
    out = yc * lax.rsqrt(var + GN_EPS) * lnw_ref[:, lanes] + lnb_ref[:, lanes] + bonus_ref[rows, lanes]
    y_ref[rows, lanes] = (out * gate_ref[rows, lanes]).astype(BF16)


def _scan_kernel(*refs):
    seq_refs, (lnw_ref, lnb_ref, state0_ref, y_ref, state_out_ref, state_ref) = refs[:10], refs[10:]

    @pl.when(pl.program_id(0) == 0)
    def _():
        state_ref[...] = state0_ref[...]

    bsz, rows_per_step = y_ref.shape[:2]
    ready = set()

    def chunk_chains(c):
        rows = slice(c * CHUNK, (c + 1) * CHUNK)
        return [_scan_chain(b * N_GROUPS + g, c, ready, rows, slice(g * SCAN_W, (g + 1) * SCAN_W),
                            *[r.at[b] for r in seq_refs], lnw_ref, lnb_ref, y_ref.at[b],
                            state_ref.at[b * N_GROUPS + g])
                for b in range(bsz) for g in range(N_GROUPS)]

    pending = [chunk_chains(c) for c in range(rows_per_step // CHUNK)]
    chains, stage = [], 0
    while chains or pending:
        if pending and stage % SCAN_STAGGER == 0:
            chains = chains + pending.pop(0)
        chains = [c for c in chains if next(c, True) is None]
        stage += 1

    @pl.when(pl.program_id(0) == pl.num_programs(0) - 1)
    def _():
        state_out_ref[...] = state_ref[...]


def _scan(seq_arrays, lnw, lnb, state0):
    bsz, seq = seq_arrays[0].shape[:2]
    rows = SCAN_CHUNKS_PER_STEP * CHUNK if seq % (SCAN_CHUNKS_PER_STEP * CHUNK) == 0 else CHUNK
    blk = pl.BlockSpec((bsz, rows, GROUP_W), lambda c: (0, c, 0))
    vec = pl.BlockSpec((1, GROUP_W), lambda c: (0, 0))
    st = pl.BlockSpec(state0.shape, lambda c: (0, 0, 0))
    return pl.pallas_call(
        _scan_kernel,
        grid=(seq // rows,),
        in_specs=[blk] * 10 + [vec, vec, st],
        out_specs=[blk, st],
        out_shape=[jax.ShapeDtypeStruct((bsz, seq, GROUP_W), BF16), jax.ShapeDtypeStruct(state0.shape, F32)],
        scratch_shapes=[pltpu.VMEM(state0.shape, F32)],
        compiler_params=_params(("arbitrary",)),
        name="scan",
    )(*seq_arrays, lnw, lnb, state0)


def _outproj_kernel(yc_ref, yr_ref, h_ref, w_ref, gpost_ref, gpre_ref, h1_ref, n2_ref, *rest, cast):
    wb_ref = w_ref
    if cast:
        wb_ref = rest[0]
        wb_ref[...] = w_ref[...].astype(BF16)

    mix = _dot(yc_ref[...], wb_ref[:GROUP_W, :]) + _dot(yr_ref[...], wb_ref[GROUP_W:, :])
    h1 = h_ref[...] + _rms(mix, gpost_ref[...])
    h1_ref[...] = h1
    n2_ref[...] = _rms(h1, gpre_ref[...]).astype(BF16)


def _outproj(yc, yr, h2d, w, gpost, gpre, tm, cast):
    m = h2d.shape[0]
    assert not cast or m == tm
    row_spec = pl.BlockSpec((tm, D_MODEL), lambda i: (i, 0))
    w_spec = pl.BlockSpec((D_MODEL, D_MODEL), lambda i: (0, 0))
    out_shape = [jax.ShapeDtypeStruct((m, D_MODEL), F32), jax.ShapeDtypeStruct((m, D_MODEL), BF16)]
    return pl.pallas_call(
        functools.partial(_outproj_kernel, cast=cast),
        grid=(m // tm,),
        in_specs=[
            pl.BlockSpec((tm, GROUP_W), lambda i: (i, 0)),
            pl.BlockSpec((tm, GROUP_W), lambda i: (i, 0)),
            pl.BlockSpec((tm, D_MODEL), lambda i: (i, 0)),
            pl.BlockSpec((D_MODEL, D_MODEL), lambda i: (0, 0), pipeline_mode=pl.Buffered(1)),
            pl.BlockSpec((1, D_MODEL), lambda i: (0, 0)),
            pl.BlockSpec((1, D_MODEL), lambda i: (0, 0)),
        ],
        out_specs=[row_spec, row_spec] + ([w_spec] if cast else []),
        out_shape=out_shape + ([jax.ShapeDtypeStruct((D_MODEL, D_MODEL), BF16)] if cast else []),
        compiler_params=_params(("parallel",)),
        name="outproj",
    )(yc, yr, h2d, w, gpost, gpre)


def _ffn_up_kernel(n2_ref, wg_ref, wu_ref, cw_ref, n2pre_ref, wd_ref, act_ref, wdb_ref, wgb_ref, wub_ref,
                   carry0_ref, carry_ref):
    b, t = pl.program_id(1), pl.program_id(2)

    @pl.when(jnp.logical_and(b == 0, t == 0))
    def _():
        wgb_ref[...] = wg_ref[...].astype(BF16)
        wub_ref[...] = wu_ref[...].astype(BF16)
        wdb_ref[...] = wd_ref[...].astype(BF16)
        pre_gate = _dot(n2pre_ref[...], wgb_ref[...])
        carry0_ref[...] = pre_gate[pre_gate.shape[0] - HALO:, :]

    @pl.when(t == 0)
    def _():
        carry_ref[...] = carry0_ref[...]

    x = n2_ref[...]
    gate = _dot(x, wgb_ref[...])
    up = _dot(x, wub_ref[...])
    xs = jnp.concatenate([carry_ref[...], gate], axis=0)
    carry_ref[...] = gate[gate.shape[0] - HALO:, :]
    cw = cw_ref[...]
    gc = pltpu.roll(xs, 2, 0)[HALO:] * cw[0:1] + pltpu.roll(xs, 1, 0)[HALO:] * cw[1:2] + gate * cw[2:3]
    act_ref[...] = (gc * jax.nn.sigmoid(gc) * up).astype(BF16)


def _ffn_up(n2, w_gate_up, conv_ffn, n2_pre, w_down):
    bsz, seq = n2.shape[:2]
    n_tiles = D_FF // TN_UP
    return pl.pallas_call(
        _ffn_up_kernel,
        grid=(n_tiles, bsz, seq // TM_UP),
        in_specs=[
            pl.BlockSpec((None, TM_UP, D_MODEL), lambda j, b, t: (b, t, 0)),
            pl.BlockSpec((D_MODEL, TN_UP), lambda j, b, t: (0, j)),
            pl.BlockSpec((D_MODEL, TN_UP), lambda j, b, t: (0, j + n_tiles)),
            pl.BlockSpec((3, TN_UP), lambda j, b, t: (0, j)),
            pl.BlockSpec(n2_pre.shape, lambda j, b, t: (0, 0)),
            pl.BlockSpec((TN_UP, D_MODEL), lambda j, b, t: (j, 0)),
        ],
        out_specs=[pl.BlockSpec((None, TM_UP, TN_UP), lambda j, b, t: (b, t, j)),
                   pl.BlockSpec((TN_UP, D_MODEL), lambda j, b, t: (j, 0))],
        out_shape=[jax.ShapeDtypeStruct((bsz, seq, D_FF), BF16), jax.ShapeDtypeStruct((D_FF, D_MODEL), BF16)],
        scratch_shapes=[pltpu.VMEM((D_MODEL, TN_UP), BF16), pltpu.VMEM((D_MODEL, TN_UP), BF16),
                        pltpu.VMEM((HALO, TN_UP), F32), pltpu.VMEM((HALO, TN_UP), F32)],
        compiler_params=_params(("arbitrary", "arbitrary", "arbitrary")),
        name="ffn_up",
    )(n2, w_gate_up, w_gate_up, conv_ffn, n2_pre, w_down)


def _ffn_down_kernel(act_ref, w_ref, h1_ref, g_ref, out_ref):
    k = pl.program_id(1)

    @pl.when(k == 0)
    def _():
        out_ref[...] = _dot(act_ref[...], w_ref[...])

    @pl.when(k > 0)
    def _():
        out_ref[...] += _dot(act_ref[...], w_ref[...])

    @pl.when(k == pl.num_programs(1) - 1)
    def _():
        def body(c, carry):
            rows = pl.ds(pl.multiple_of(c * EPI_ROWS, EPI_ROWS), EPI_ROWS)
            out_ref[rows, :] = h1_ref[rows, :] + _rms(out_ref[rows, :], g_ref[...])
            return carry

        lax.fori_loop(0, out_ref.shape[0] // EPI_ROWS, body, 0)


def _ffn_down(act, w, h1, g):
    m = act.shape[0]
    tm = TM_DOWN
    return pl.pallas_call(
        _ffn_down_kernel,
        grid=(m // tm, D_FF // TK_DOWN),
        in_specs=[
            pl.BlockSpec((tm, TK_DOWN), lambda i, k: (i, k)),
            pl.BlockSpec((TK_DOWN, D_MODEL), lambda i, k: (k, 0)),
            pl.BlockSpec((tm, D_MODEL), lambda i, k: (i, 0)),
            pl.BlockSpec((1, D_MODEL), lambda i, k: (0, 0)),
        ],
        out_specs=pl.BlockSpec((tm, D_MODEL), lambda i, k: (i, 0)),
        out_shape=jax.ShapeDtypeStruct((m, D_MODEL), F32),
        compiler_params=_params(("parallel", "arbitrary")),
        name="ffn_down",
    )(act, w, h1, g)


def kernel(x, meta_tokens, g_pre_mix, w_in, conv_w, mu_rwkv, w_decay_up, w0, a_up, a0, g_up, k_k, k_a,
           r_k, ln_x_w, ln_x_b, w_out, g_post_mix, g_pre_ffn, w_gate_up, conv_ffn, w_down, g_post_ffn):
    bsz, seq, _ = x.shape
    assert x.shape[2] == D_MODEL and meta_tokens.shape == (N_META, D_MODEL)
    row = lambda p: p.reshape(1, -1).astype(F32)
    g3 = 3 * GROUP_W

    def window_rows(w, start, win):
        return jnp.pad(w, ((start - win[0], win[1] - start - w.shape[0]), (0, 0))).astype(BF16)

    mu_r, mu_l = row(mu_rwkv[:g3]), row(jnp.pad(mu_rwkv[g3:], (0, LORA_PAD - LORA_W)))
    wdec = window_rows(w_decay_up, 0, WD_WIN)
    aup = window_rows(a_up, DECAY_LORA, AD_WIN)
    gup = window_rows(g_up, DECAY_LORA + AAA_LORA, GD_WIN)
    head_of = jnp.arange(GROUP_W) // HEAD_N
    ones_bd = (head_of[:, None] == head_of[None, :]).astype(BF16)
    prep_consts = (conv_w.astype(F32), mu_r, mu_l, row(w0), row(a0), row(k_k), row(k_a), row(r_k),
                   wdec, aup, gup, ones_bd)
    gpre, gpost, gffn = row(g_pre_mix), row(g_post_mix), row(g_pre_ffn)
    lnw, lnb = row(ln_x_w), row(ln_x_b)

    pre = jnp.concatenate([jnp.zeros((PREFIX - N_META, D_MODEL), F32), meta_tokens.astype(F32)], axis=0)
    z_pre, w_in_b = _inproj(pre, gpre, w_in, PREFIX, TN_PROJ_PRE, cast=True)
    pre_out = _prep(z_pre[None], jnp.zeros((HALO, IN_COLS_PAD), F32), *prep_consts, PREFIX)
    y_pre, state_pre = _scan(pre_out[1:], lnw, lnb, jnp.zeros((N_GROUPS, SCAN_W, SCAN_W), F32))
    _, n2_pre, w_out_b = _outproj(pre_out[0][0], y_pre[0], pre, w_out, gpost, gffn, PREFIX, cast=True)

    m = bsz * seq
    x2d = x.reshape(m, D_MODEL)
    z = _inproj(x2d, gpre, w_in_b, TM_PROJ, TN_PROJ, cast=False).reshape(bsz, seq, IN_COLS_PAD)
    main_out = _prep(z, z_pre[PREFIX - HALO:], *prep_consts, TM_PREP)
    y_rwkv, _ = _scan(main_out[1:], lnw, lnb, jnp.tile(state_pre, (bsz, 1, 1)))
    h1, n2 = _outproj(main_out[0].reshape(m, GROUP_W), y_rwkv.reshape(m, GROUP_W), x2d, w_out_b, gpost, gffn,
                      TM_OUT, cast=False)
    act, w_down_b = _ffn_up(n2.reshape(bsz, seq, D_MODEL), w_gate_up, conv_ffn.astype(F32),
                            n2_pre[PREFIX - N_META:], w_down)
    out = _ffn_down(act.reshape(m, D_FF), w_down_b, h1, row(g_post_ffn))
    return out.reshape(bsz, seq, D_MODEL)
```

```python
import functools
import math

import jax
import jax.numpy as jnp
from jax import lax
from jax.experimental import pallas as pl
from jax.experimental.pallas import tpu as pltpu

F32 = jnp.float32
BF16 = jnp.bfloat16

D_MODEL = 2048
N_META = 16
GROUP_W = 1024
HEAD_N = 64
DECAY_LORA = 96
AAA_LORA = 96
GATE_LORA = 256
LANE = 128
LORA_W = DECAY_LORA + AAA_LORA + GATE_LORA
LORA_PAD = 512
WD_WIN = (0, LANE)
AD_WIN = (0, 2 * LANE)
GD_WIN = (LANE, LORA_PAD)
IN_COLS = 3 * GROUP_W + 3 * GROUP_W + LORA_W
IN_COLS_PAD = IN_COLS - LORA_W + LORA_PAD
D_FF = 5632
RMS_EPS = 1e-6
GN_EPS = 64e-5
DECAY_SCALE = math.exp(-0.5)

CHUNK = 64
assert CHUNK == HEAD_N
PREFIX = CHUNK
HALO = 8
HEADS_PER_GROUP = 4
SCAN_W = HEADS_PER_GROUP * HEAD_N
N_GROUPS = GROUP_W // SCAN_W
SCAN_CHUNKS_PER_STEP = 4
SCAN_STAGGER = 2

VMEM_LIMIT = 56 * 1024 * 1024

TM_PROJ = 1024
TN_PROJ = IN_COLS_PAD // 4
TN_PROJ_PRE = 512
TM_PREP = 256
TM_OUT = 512
TM_UP = 1024
TN_UP = 512
TM_DOWN = 1024
TK_DOWN = D_FF // 4
EPI_ROWS = 128


def _dot(a, b):
    return jnp.dot(a, b, preferred_element_type=F32)


def _dot_nt(a, b):
    return lax.dot_general(a, b, (((1,), (1,)), ((), ())), preferred_element_type=F32)


def _dot_tn(a, b):
    return lax.dot_general(a, b, (((0,), (0,)), ((), ())), preferred_element_type=F32)


def _rms(x, g):
    return x * lax.rsqrt(jnp.mean(x * x, axis=-1, keepdims=True) + RMS_EPS) * g


def _params(sem):
    return pltpu.CompilerParams(dimension_semantics=sem, vmem_limit_bytes=VMEM_LIMIT)


def _inproj_kernel(h_ref, g_ref, w_ref, z_ref, *rest, cast):
    xn_ref = rest[-1]

    @pl.when(pl.program_id(1) == 0)
    def _():
        xn_ref[...] = _rms(h_ref[...], g_ref[...]).astype(BF16)

    if cast:
        tn = w_ref.shape[0]
        col = pl.program_id(1) * tn + lax.broadcasted_iota(jnp.int32, (tn, 1), 0)
        rest[0][...] = jnp.where(col < IN_COLS, w_ref[...], 0.0).T.astype(BF16)
        z_ref[...] = _dot(xn_ref[...], rest[0][...])
    else:
        z_ref[...] = _dot(xn_ref[...], w_ref[...])


def _inproj(h2d, g, w, tm, tn, cast):
    m = h2d.shape[0]
    z_spec = pl.BlockSpec((tm, tn), lambda i, j: (i, j))
    z_shape = jax.ShapeDtypeStruct((m, IN_COLS_PAD), F32)
    w_spec = pl.BlockSpec((D_MODEL, tn), lambda i, j: (0, j))
    return pl.pallas_call(
        functools.partial(_inproj_kernel, cast=cast),
        grid=(m // tm, IN_COLS_PAD // tn),
        in_specs=[
            pl.BlockSpec((tm, D_MODEL), lambda i, j: (i, 0)),
            pl.BlockSpec((1, D_MODEL), lambda i, j: (0, 0)),
            pl.BlockSpec((tn, D_MODEL), lambda i, j: (j, 0)) if cast else w_spec,
        ],
        out_specs=[z_spec, w_spec] if cast else z_spec,
        out_shape=[z_shape, jax.ShapeDtypeStruct((D_MODEL, IN_COLS_PAD), BF16)] if cast else z_shape,
        scratch_shapes=[pltpu.VMEM((tm, D_MODEL), BF16)],
        compiler_params=_params(("parallel", "arbitrary")),
        name="inproj",
    )(h2d, g, w)


def _prev_rows(cur, halo, shift):
    xs = jnp.concatenate([halo, cur], axis=0)
    return pltpu.roll(xs, shift, 0)[HALO:]


def _split_dot(m01, x):
    x1 = x.astype(BF16)
    r1 = x - x1.astype(F32)
    x2 = r1.astype(BF16)
    x3 = (r1 - x2.astype(F32)).astype(BF16)
    return _dot(m01, x1) + _dot(m01, x2) + _dot(m01, x3)


def _prep_kernel(zc_ref, zr_ref, zl_ref, hc_ref, hr_ref, hl_ref, ic_ref, ir_ref, il_ref,
                 convw_ref, mur_ref, mul_ref, w0_ref, a0_ref, kk_ref, ka_ref, rk_ref,
                 wdec_ref, aup_ref, gup_ref, ones_ref, tri_ref,
                 yconv_o, at_o, rt_o, bt_o, kt_o, v_o, w_o, bonus_o, gate_o):
    first = pl.program_id(1) == 0
    g = GROUP_W
    hc = jnp.where(first, ic_ref[...], hc_ref[...])
    hr = jnp.where(first, ir_ref[...], hr_ref[...])
    hl = jnp.where(first, il_ref[...], hl_ref[...])

    zc = zc_ref[...]
    p = zc[:, g:2 * g] * zc[:, 2 * g:]
    ph = hc[:, g:2 * g] * hc[:, 2 * g:]
    cw = convw_ref[...]
    conv = _prev_rows(p, ph, 2) * cw[0:1] + _prev_rows(p, ph, 1) * cw[1:2] + p * cw[2:3]
    yconv_o[...] = (zc[:, :g] * conv).astype(BF16)

    zr = zr_ref[...]
    zs = zr + (_prev_rows(zr, hr, 1) - zr) * mur_ref[...]
    zl = zl_ref[...]
    zsl = zl + (_prev_rows(zl, hl, 1) - zl) * mul_ref[...]
    r = zs[:, :g]
    k = zs[:, g:2 * g]
    v = zs[:, 2 * g:]

    wd = jnp.tanh(zsl[:, WD_WIN[0]:WD_WIN[1]]).astype(BF16)
    log_w = -DECAY_SCALE * jax.nn.sigmoid(w0_ref[...] + _dot(wd, wdec_ref[...]))
    a = jax.nn.sigmoid(a0_ref[...] + _dot(zsl[:, AD_WIN[0]:AD_WIN[1]].astype(BF16), aup_ref[...]))
    gate_o[...] = _dot(jax.nn.sigmoid(zsl[:, GD_WIN[0]:GD_WIN[1]]).astype(BF16), gup_ref[...]).astype(BF16)

    ones_bd = ones_ref[...]
    kk = k * kk_ref[...]
    kk = kk / jnp.maximum(jnp.sqrt(_dot((kk * kk).astype(BF16), ones_bd)), 1e-12)
    k2 = k * (1.0 + (a - 1.0) * ka_ref[...])
    bonus_o[...] = (_dot((r * k2 * rk_ref[...]).astype(BF16), ones_bd) * v).astype(BF16)
    bv = kk * a

    cum = _split_dot(tri_ref[...], log_w)
    w_inc = jnp.exp(cum)
    w_inv = jnp.exp(-cum)
    for c in range(w_o.shape[0]):
        w_o[c] = w_inc[(c + 1) * CHUNK - 1:(c + 1) * CHUNK, :]
    at_o[...] = (-kk * jnp.exp(cum - log_w)).astype(BF16)
    rt_o[...] = (r * w_inc).astype(BF16)
    bt_o[...] = (bv * w_inv).astype(BF16)
    kt_o[...] = (k2 * w_inv).astype(BF16)
    v_o[...] = v.astype(BF16)


def _prep(z, z_init, convw, mur, mul, w0, a0, k_k, k_a, r_k, wdec, aup, gup, ones_bd, tm):
    bsz, seq = z.shape[:2]
    g3 = 3 * GROUP_W
    lora_blk = 2 * g3 // LORA_PAD
    t_of = jnp.arange(tm)
    same_chunk = (t_of[:, None] // CHUNK) == (t_of[None, :] // CHUNK)
    tri = jnp.logical_and(same_chunk, t_of[None, :] <= t_of[:, None]).astype(BF16)

    halo_idx = lambda i: jnp.maximum(i * (tm // HALO) - 1, 0)
    row = lambda w, c: pl.BlockSpec((None, tm, w), lambda b, i: (b, i, c))
    halo = lambda w, c: pl.BlockSpec((None, HALO, w), lambda b, i: (b, halo_idx(i), c))
    init = lambda w, c: pl.BlockSpec((HALO, w), lambda b, i: (0, c))
    full = lambda a: pl.BlockSpec(a.shape, lambda b, i: (0,) * a.ndim)
    consts = (convw, mur, mul, w0, a0, k_k, k_a, r_k, wdec, aup, gup, ones_bd, tri)
    out_bf = jax.ShapeDtypeStruct((bsz, seq, GROUP_W), BF16)
    out_spec = pl.BlockSpec((None, tm, GROUP_W), lambda b, i: (b, i, 0))
    w_shape = jax.ShapeDtypeStruct((bsz, seq // CHUNK, 1, GROUP_W), F32)
    w_spec = pl.BlockSpec((None, tm // CHUNK, 1, GROUP_W), lambda b, i: (b, i, 0, 0))
    return pl.pallas_call(
        _prep_kernel,
        grid=(bsz, seq // tm),
        in_specs=[row(g3, 0), row(g3, 1), row(LORA_PAD, lora_blk),
                  halo(g3, 0), halo(g3, 1), halo(LORA_PAD, lora_blk),
                  init(g3, 0), init(g3, 1), init(LORA_PAD, lora_blk)] + [full(c) for c in consts],
        out_specs=[out_spec] * 6 + [w_spec] + [out_spec] * 2,
        out_shape=[out_bf] * 6 + [w_shape] + [out_bf] * 2,
        compiler_params=_params(("parallel", "arbitrary")),
        name="prep",
    )(z, z, z, z, z, z, z_init, z_init, z_init, *consts)


def _block_diag(x, mask):
    return jnp.where(mask, jnp.concatenate([x] * HEADS_PER_GROUP, axis=0), 0.0).astype(BF16)


def _scan_chain(chain_id, chunk, ready, rows, lanes, at_ref, rt_ref, bt_ref, kt_ref, v_ref, w_ref,
                bonus_ref, gate_ref, lnw_ref, lnb_ref, y_ref, state_ref):
    f = lambda ref: ref[rows, lanes].astype(F32)
    at, rt, bt, kt, v = map(f, (at_ref, rt_ref, bt_ref, kt_ref, v_ref))

    shape_bd = (SCAN_W, SCAN_W)
    bd_mask = (lax.broadcasted_iota(jnp.int32, shape_bd, 0) // HEAD_N
               == lax.broadcasted_iota(jnp.int32, shape_bd, 1) // HEAD_N)
    shape_c = (CHUNK, SCAN_W)
    t_idx = lax.broadcasted_iota(jnp.int32, shape_c, 0)
    s_idx = lax.broadcasted_iota(jnp.int32, shape_c, 1) % CHUNK
    bd = lambda x: _block_diag(x, bd_mask)
    stack = lambda a, b: jnp.concatenate([a, b], axis=0).astype(BF16)

    ar = stack(at, rt)
    ab = _dot_nt(ar, bd(bt))
    ak = _dot_nt(ar, bd(kt))
    yield
    a_ab = jnp.where(s_idx < t_idx, ab[:CHUNK], 0.0)
    a_rb = jnp.where(s_idx <= t_idx, ab[CHUNK:], 0.0)
    a_ak = jnp.where(s_idx < t_idx, ak[:CHUNK], 0.0)
    a_rk = jnp.where(s_idx <= t_idx, ak[CHUNK:], 0.0)

    inv = jnp.where(s_idx == t_idx, 1.0, 0.0) + a_ab
    pw = _dot(a_ab.astype(BF16), bd(a_ab))
    akv = _dot(stack(a_ak, a_rk), bd(v))
    yield
    for _ in range(int(math.log2(CHUNK)) - 2):
        both = _dot(stack(pw, inv), bd(pw))
        yield
        pw = both[:CHUNK]
        inv = inv + both[CHUNK:]
    inv = inv + _dot(inv.astype(BF16), bd(pw))
    assert chunk == 0 or (chain_id, chunk - 1) in ready
    state = state_ref[...]
    ars = _dot_nt(ar, state.astype(BF16))
    yield

    u = _dot(inv.astype(BF16), bd(ars[:CHUNK] + akv[:CHUNK]))
    yield
    y = ars[CHUNK:] + akv[CHUNK:] + _dot(a_rb.astype(BF16), bd(u))
    state_ref[...] = (state + jnp.where(bd_mask, _dot_tn(stack(u, v), stack(bt, kt)), 0.0)) * w_ref[chunk, :, lanes]
    ready.add((chain_id, chunk))
    yield

    ones_bd = jnp.where(bd_mask, 1.0, 0.0).astype(BF16)
    mean = _dot(y.astype(BF16), ones_bd) * (1.0 / HEAD_N)
    yield
    yc = y - mean
    var = _dot((yc * yc).astype(BF16), ones_bd) * (1.0 / HEAD_N)
    yield
    out = yc * lax.rsqrt(var + GN_EPS) * lnw_ref[:, lanes] + lnb_ref[:, lanes] + bonus_ref[rows, lanes]
    y_ref[rows, lanes] = (out * gate_ref[rows, lanes]).astype(BF16)


def _scan_kernel(*refs):
    seq_refs, (lnw_ref, lnb_ref, state0_ref, y_ref, state_out_ref, state_ref) = refs[:8], refs[8:]

    @pl.when(pl.program_id(0) == 0)
    def _():
        state_ref[...] = state0_ref[...]

    bsz, rows_per_step = y_ref.shape[:2]
    ready = set()

    def chunk_chains(c):
        rows = slice(c * CHUNK, (c + 1) * CHUNK)
        return [_scan_chain(b * N_GROUPS + g, c, ready, rows, slice(g * SCAN_W, (g + 1) * SCAN_W),
                            *[r.at[b] for r in seq_refs], lnw_ref, lnb_ref, y_ref.at[b],
                            state_ref.at[b * N_GROUPS + g])
                for b in range(bsz) for g in range(N_GROUPS)]

    pending = [chunk_chains(c) for c in range(rows_per_step // CHUNK)]
    chains, stage = [], 0
    while chains or pending:
        if pending and stage % SCAN_STAGGER == 0:
            chains = chains + pending.pop(0)
        chains = [c for c in chains if next(c, True) is None]
        stage += 1

    @pl.when(pl.program_id(0) == pl.num_programs(0) - 1)
    def _():
        state_out_ref[...] = state_ref[...]


def _scan(seq_arrays, lnw, lnb, state0):
    bsz, seq = seq_arrays[0].shape[:2]
    rows = SCAN_CHUNKS_PER_STEP * CHUNK if seq % (SCAN_CHUNKS_PER_STEP * CHUNK) == 0 else CHUNK
    blk = pl.BlockSpec((bsz, rows, GROUP_W), lambda c: (0, c, 0))
    wblk = pl.BlockSpec((bsz, rows // CHUNK, 1, GROUP_W), lambda c: (0, c, 0, 0))
    vec = pl.BlockSpec((1, GROUP_W), lambda c: (0, 0))
    st = pl.BlockSpec(state0.shape, lambda c: (0, 0, 0))
    return pl.pallas_call(
        _scan_kernel,
        grid=(seq // rows,),
        in_specs=[blk] * 5 + [wblk] + [blk] * 2 + [vec, vec, st],
        out_specs=[blk, st],
        out_shape=[jax.ShapeDtypeStruct((bsz, seq, GROUP_W), BF16), jax.ShapeDtypeStruct(state0.shape, F32)],
        scratch_shapes=[pltpu.VMEM(state0.shape, F32)],
        compiler_params=_params(("arbitrary",)),
        name="scan",
    )(*seq_arrays, lnw, lnb, state0)


def _outproj_kernel(yc_ref, yr_ref, h_ref, w_ref, gpost_ref, gpre_ref, h1_ref, n2_ref, *rest, cast):
    wb_ref = w_ref
    if cast:
        wb_ref = rest[0]
        wb_ref[...] = w_ref[...].astype(BF16)

    mix = _dot(yc_ref[...], wb_ref[:GROUP_W, :]) + _dot(yr_ref[...], wb_ref[GROUP_W:, :])
    h1 = h_ref[...] + _rms(mix, gpost_ref[...])
    h1_ref[...] = h1
    n2_ref[...] = _rms(h1, gpre_ref[...]).astype(BF16)


def _outproj(yc, yr, h2d, w, gpost, gpre, tm, cast):
    m = h2d.shape[0]
    assert not cast or m == tm
    row_spec = pl.BlockSpec((tm, D_MODEL), lambda i: (i, 0))
    w_spec = pl.BlockSpec((D_MODEL, D_MODEL), lambda i: (0, 0))
    out_shape = [jax.ShapeDtypeStruct((m, D_MODEL), F32), jax.ShapeDtypeStruct((m, D_MODEL), BF16)]
    return pl.pallas_call(
        functools.partial(_outproj_kernel, cast=cast),
        grid=(m // tm,),
        in_specs=[
            pl.BlockSpec((tm, GROUP_W), lambda i: (i, 0)),
            pl.BlockSpec((tm, GROUP_W), lambda i: (i, 0)),
            pl.BlockSpec((tm, D_MODEL), lambda i: (i, 0)),
            pl.BlockSpec((D_MODEL, D_MODEL), lambda i: (0, 0), pipeline_mode=pl.Buffered(1)),
            pl.BlockSpec((1, D_MODEL), lambda i: (0, 0)),
            pl.BlockSpec((1, D_MODEL), lambda i: (0, 0)),
        ],
        out_specs=[row_spec, row_spec] + ([w_spec] if cast else []),
        out_shape=out_shape + ([jax.ShapeDtypeStruct((D_MODEL, D_MODEL), BF16)] if cast else []),
        compiler_params=_params(("parallel",)),
        name="outproj",
    )(yc, yr, h2d, w, gpost, gpre)


def _ffn_up_kernel(n2_ref, wg_ref, wu_ref, cw_ref, n2pre_ref, wd_ref, act_ref, wdb_ref, wgb_ref, wub_ref,
                   carry0_ref, carry_ref):
    b, t = pl.program_id(1), pl.program_id(2)

    @pl.when(jnp.logical_and(b == 0, t == 0))
    def _():
        wgb_ref[...] = wg_ref[...].astype(BF16)
        wub_ref[...] = wu_ref[...].astype(BF16)
        wdb_ref[...] = wd_ref[...].astype(BF16)
        pre_gate = _dot(n2pre_ref[...], wgb_ref[...])
        carry0_ref[...] = pre_gate[pre_gate.shape[0] - HALO:, :]

    @pl.when(t == 0)
    def _():
        carry_ref[...] = carry0_ref[...]

    x = n2_ref[...]
    gate = _dot(x, wgb_ref[...])
    up = _dot(x, wub_ref[...])
    xs = jnp.concatenate([carry_ref[...], gate], axis=0)
    carry_ref[...] = gate[gate.shape[0] - HALO:, :]
    cw = cw_ref[...]
    gc = pltpu.roll(xs, 2, 0)[HALO:] * cw[0:1] + pltpu.roll(xs, 1, 0)[HALO:] * cw[1:2] + gate * cw[2:3]
    act_ref[...] = (gc * jax.nn.sigmoid(gc) * up).astype(BF16)


def _ffn_up(n2, w_gate_up, conv_ffn, n2_pre, w_down):
    bsz, seq = n2.shape[:2]
    n_tiles = D_FF // TN_UP
    return pl.pallas_call(
        _ffn_up_kernel,
        grid=(n_tiles, bsz, seq // TM_UP),
        in_specs=[
            pl.BlockSpec((None, TM_UP, D_MODEL), lambda j, b, t: (b, t, 0)),
            pl.BlockSpec((D_MODEL, TN_UP), lambda j, b, t: (0, j)),
            pl.BlockSpec((D_MODEL, TN_UP), lambda j, b, t: (0, j + n_tiles)),
            pl.BlockSpec((3, TN_UP), lambda j, b, t: (0, j)),
            pl.BlockSpec(n2_pre.shape, lambda j, b, t: (0, 0)),
            pl.BlockSpec((TN_UP, D_MODEL), lambda j, b, t: (j, 0)),
        ],
        out_specs=[pl.BlockSpec((None, TM_UP, TN_UP), lambda j, b, t: (b, t, j)),
                   pl.BlockSpec((TN_UP, D_MODEL), lambda j, b, t: (j, 0))],
        out_shape=[jax.ShapeDtypeStruct((bsz, seq, D_FF), BF16), jax.ShapeDtypeStruct((D_FF, D_MODEL), BF16)],
        scratch_shapes=[pltpu.VMEM((D_MODEL, TN_UP), BF16), pltpu.VMEM((D_MODEL, TN_UP), BF16),
                        pltpu.VMEM((HALO, TN_UP), F32), pltpu.VMEM((HALO, TN_UP), F32)],
        compiler_params=_params(("arbitrary", "arbitrary", "arbitrary")),
        name="ffn_up",
    )(n2, w_gate_up, w_gate_up, conv_ffn, n2_pre, w_down)


def _ffn_down_kernel(act_ref, w_ref, h1_ref, g_ref, out_ref):
    k = pl.program_id(1)

    @pl.when(k == 0)
    def _():
        out_ref[...] = _dot(act_ref[...], w_ref[...])

    @pl.when(k > 0)
    def _():
        out_ref[...] += _dot(act_ref[...], w_ref[...])

    @pl.when(k == pl.num_programs(1) - 1)
    def _():
        def body(c, carry):
            rows = pl.ds(pl.multiple_of(c * EPI_ROWS, EPI_ROWS), EPI_ROWS)
            out_ref[rows, :] = h1_ref[rows, :] + _rms(out_ref[rows, :], g_ref[...])
            return carry

        lax.fori_loop(0, out_ref.shape[0] // EPI_ROWS, body, 0)


def _ffn_down(act, w, h1, g):
    m = act.shape[0]
    tm = TM_DOWN
    return pl.pallas_call(
        _ffn_down_kernel,
        grid=(m // tm, D_FF // TK_DOWN),
        in_specs=[
            pl.BlockSpec((tm, TK_DOWN), lambda i, k: (i, k)),
            pl.BlockSpec((TK_DOWN, D_MODEL), lambda i, k: (k, 0)),
            pl.BlockSpec((tm, D_MODEL), lambda i, k: (i, 0)),
            pl.BlockSpec((1, D_MODEL), lambda i, k: (0, 0)),
        ],
        out_specs=pl.BlockSpec((tm, D_MODEL), lambda i, k: (i, 0)),
        out_shape=jax.ShapeDtypeStruct((m, D_MODEL), F32),
        compiler_params=_params(("parallel", "arbitrary")),
        name="ffn_down",
    )(act, w, h1, g)


def kernel(x, meta_tokens, g_pre_mix, w_in, conv_w, mu_rwkv, w_decay_up, w0, a_up, a0, g_up, k_k, k_a,
           r_k, ln_x_w, ln_x_b, w_out, g_post_mix, g_pre_ffn, w_gate_up, conv_ffn, w_down, g_post_ffn):
    bsz, seq, _ = x.shape
    assert x.shape[2] == D_MODEL and meta_tokens.shape == (N_META, D_MODEL)
    row = lambda p: p.reshape(1, -1).astype(F32)
    g3 = 3 * GROUP_W

    def window_rows(w, start, win):
        return jnp.pad(w, ((start - win[0], win[1] - start - w.shape[0]), (0, 0))).astype(BF16)

    mu_r, mu_l = row(mu_rwkv[:g3]), row(jnp.pad(mu_rwkv[g3:], (0, LORA_PAD - LORA_W)))
    wdec = window_rows(w_decay_up, 0, WD_WIN)
    aup = window_rows(a_up, DECAY_LORA, AD_WIN)
    gup = window_rows(g_up, DECAY_LORA + AAA_LORA, GD_WIN)
    head_of = jnp.arange(GROUP_W) // HEAD_N
    ones_bd = (head_of[:, None] == head_of[None, :]).astype(BF16)
    prep_consts = (conv_w.astype(F32), mu_r, mu_l, row(w0), row(a0), row(k_k), row(k_a), row(r_k),
                   wdec, aup, gup, ones_bd)
    gpre, gpost, gffn = row(g_pre_mix), row(g_post_mix), row(g_pre_ffn)
    lnw, lnb = row(ln_x_w), row(ln_x_b)

    pre = jnp.concatenate([jnp.zeros((PREFIX - N_META, D_MODEL), F32), meta_tokens.astype(F32)], axis=0)
    z_pre, w_in_b = _inproj(pre, gpre, w_in.T, PREFIX, TN_PROJ_PRE, cast=True)
    pre_out = _prep(z_pre[None], jnp.zeros((HALO, IN_COLS_PAD), F32), *prep_consts, PREFIX)
    y_pre, state_pre = _scan(pre_out[1:], lnw, lnb, jnp.zeros((N_GROUPS, SCAN_W, SCAN_W), F32))
    _, n2_pre, w_out_b = _outproj(pre_out[0][0], y_pre[0], pre, w_out, gpost, gffn, PREFIX, cast=True)

    m = bsz * seq
    x2d = x.reshape(m, D_MODEL)
    z = _inproj(x2d, gpre, w_in_b, TM_PROJ, TN_PROJ, cast=False).reshape(bsz, seq, IN_COLS_PAD)
    main_out = _prep(z, z_pre[PREFIX - HALO:], *prep_consts, TM_PREP)
    y_rwkv, _ = _scan(main_out[1:], lnw, lnb, jnp.tile(state_pre, (bsz, 1, 1)))
    h1, n2 = _outproj(main_out[0].reshape(m, GROUP_W), y_rwkv.reshape(m, GROUP_W), x2d, w_out_b, gpost, gffn,
                      TM_OUT, cast=False)
    act, w_down_b = _ffn_up(n2.reshape(bsz, seq, D_MODEL), w_gate_up, conv_ffn.astype(F32),
                            n2_pre[PREFIX - N_META:], w_down)
    out = _ffn_down(act.reshape(m, D_FF), w_down_b, h1, row(g_post_ffn))
    return out.reshape(bsz, seq, D_MODEL)
```

```python
import functools
import math

import jax
import jax.numpy as jnp
from jax import lax
from jax.experimental import pallas as pl
from jax.experimental.pallas import tpu as pltpu

F32 = jnp.float32
BF16 = jnp.bfloat16

D_MODEL = 2048
N_META = 16
GROUP_W = 1024
HEAD_N = 64
DECAY_LORA = 96
AAA_LORA = 96
GATE_LORA = 256
LANE = 128
LORA_W = DECAY_LORA + AAA_LORA + GATE_LORA
LORA_PAD = 512
WD_WIN = (0, LANE)
AD_WIN = (0, 2 * LANE)
GD_WIN = (LANE, LORA_PAD)
IN_COLS = 3 * GROUP_W + 3 * GROUP_W + LORA_W
IN_COLS_PAD = IN_COLS - LORA_W + LORA_PAD
D_FF = 5632
RMS_EPS = 1e-6
GN_EPS = 64e-5
DECAY_SCALE = math.exp(-0.5)

CHUNK = 64
assert CHUNK == HEAD_N
PREFIX = CHUNK
HALO = 8
HEADS_PER_GROUP = 4
SCAN_W = HEADS_PER_GROUP * HEAD_N
N_GROUPS = GROUP_W // SCAN_W
SCAN_CHUNKS_PER_STEP = 4
SCAN_STAGGER = 2

VMEM_LIMIT = 56 * 1024 * 1024

TM_PROJ = 1024
TN_PROJ = IN_COLS_PAD // 4
TN_PROJ_PRE = 512
TM_PREP = 256
PREP_SLAB = 256
TM_OUT = 512
TM_UP = 1024
TN_UP = 512
TM_DOWN = 1024
TK_DOWN = D_FF // 4
EPI_ROWS = 128


def _dot(a, b):
    return jnp.dot(a, b, preferred_element_type=F32)


def _dot_nt(a, b):
    return lax.dot_general(a, b, (((1,), (1,)), ((), ())), preferred_element_type=F32)


def _dot_tn(a, b):
    return lax.dot_general(a, b, (((0,), (0,)), ((), ())), preferred_element_type=F32)


def _rms(x, g):
    return x * lax.rsqrt(jnp.mean(x * x, axis=-1, keepdims=True) + RMS_EPS) * g


def _params(sem):
    return pltpu.CompilerParams(dimension_semantics=sem, vmem_limit_bytes=VMEM_LIMIT)


def _inproj_kernel(h_ref, g_ref, w_ref, z_ref, *rest, cast):
    xn_ref = rest[-1]

    @pl.when(pl.program_id(1) == 0)
    def _():
        xn_ref[...] = _rms(h_ref[...], g_ref[...]).astype(BF16)

    if cast:
        tn = w_ref.shape[0]
        col = pl.program_id(1) * tn + lax.broadcasted_iota(jnp.int32, (tn, 1), 0)
        rest[0][...] = jnp.where(col < IN_COLS, w_ref[...], 0.0).T.astype(BF16)
        z_ref[...] = _dot(xn_ref[...], rest[0][...])
    else:
        z_ref[...] = _dot(xn_ref[...], w_ref[...])


def _inproj(h2d, g, w, tm, tn, cast):
    m = h2d.shape[0]
    z_spec = pl.BlockSpec((tm, tn), lambda i, j: (i, j))
    z_shape = jax.ShapeDtypeStruct((m, IN_COLS_PAD), F32)
    w_spec = pl.BlockSpec((D_MODEL, tn), lambda i, j: (0, j))
    return pl.pallas_call(
        functools.partial(_inproj_kernel, cast=cast),
        grid=(m // tm, IN_COLS_PAD // tn),
        in_specs=[
            pl.BlockSpec((tm, D_MODEL), lambda i, j: (i, 0)),
            pl.BlockSpec((1, D_MODEL), lambda i, j: (0, 0)),
            pl.BlockSpec((tn, D_MODEL), lambda i, j: (j, 0)) if cast else w_spec,
        ],
        out_specs=[z_spec, w_spec] if cast else z_spec,
        out_shape=[z_shape, jax.ShapeDtypeStruct((D_MODEL, IN_COLS_PAD), BF16)] if cast else z_shape,
        scratch_shapes=[pltpu.VMEM((tm, D_MODEL), BF16)],
        compiler_params=_params(("parallel", "arbitrary")),
        name="inproj",
    )(h2d, g, w)


def _prev_rows(cur, halo, shift):
    xs = jnp.concatenate([halo, cur], axis=0)
    return pltpu.roll(xs, shift, 0)[HALO:]


def _split_dot(m01, x):
    x1 = x.astype(BF16)
    r1 = x - x1.astype(F32)
    x2 = r1.astype(BF16)
    x3 = (r1 - x2.astype(F32)).astype(BF16)
    return _dot(m01, x1) + _dot(m01, x2) + _dot(m01, x3)


def _prep_kernel(zc_ref, zr_ref, zl_ref, hc_ref, hr_ref, hl_ref, ic_ref, ir_ref, il_ref,
                 convw_ref, mur_ref, mul_ref, w0_ref, a0_ref, kk_ref, ka_ref, rk_ref,
                 wdec_ref, aup_ref, gup_ref, ones_ref, tri_ref,
                 yconv_o, at_o, rt_o, bt_o, kt_o, v_o, w_o, bonus_o, gate_o):
    first = pl.program_id(1) == 0
    g, w = GROUP_W, PREP_SLAB
    halo_of = lambda init_ref, halo_ref, c: jnp.where(first, init_ref[:, c], halo_ref[:, c])

    zl = zl_ref[...]
    zsl = zl + (_prev_rows(zl, halo_of(il_ref, hl_ref, slice(None)), 1) - zl) * mul_ref[...]
    wd = jnp.tanh(zsl[:, WD_WIN[0]:WD_WIN[1]]).astype(BF16)
    ad = zsl[:, AD_WIN[0]:AD_WIN[1]].astype(BF16)
    gd = jax.nn.sigmoid(zsl[:, GD_WIN[0]:GD_WIN[1]]).astype(BF16)
    ones_bd = ones_ref[...]
    tri = tri_ref[...]
    cw = convw_ref[...]

    for s in range(g // w):
        ln = slice(s * w, (s + 1) * w)
        col = lambda sec: slice(sec * g + s * w, sec * g + (s + 1) * w)

        p = zc_ref[:, col(1)] * zc_ref[:, col(2)]
        ph = halo_of(ic_ref, hc_ref, col(1)) * halo_of(ic_ref, hc_ref, col(2))
        conv = _prev_rows(p, ph, 2) * cw[0:1, ln] + _prev_rows(p, ph, 1) * cw[1:2, ln] + p * cw[2:3, ln]
        yconv_o[:, ln] = (zc_ref[:, col(0)] * conv).astype(BF16)

        def shifted(sec):
            z = zr_ref[:, col(sec)]
            return z + (_prev_rows(z, halo_of(ir_ref, hr_ref, col(sec)), 1) - z) * mur_ref[:, col(sec)]

        r, k, v = shifted(0), shifted(1), shifted(2)
        log_w = -DECAY_SCALE * jax.nn.sigmoid(w0_ref[:, ln] + _dot(wd, wdec_ref[:, ln]))
        a = jax.nn.sigmoid(a0_ref[:, ln] + _dot(ad, aup_ref[:, ln]))
        gate_o[:, ln] = _dot(gd, gup_ref[:, ln]).astype(BF16)

        kk = k * kk_ref[:, ln]
        kk = kk * lax.rsqrt(jnp.maximum(_dot((kk * kk).astype(BF16), ones_bd), 1e-24))
        k2 = k * (1.0 + (a - 1.0) * ka_ref[:, ln])
        bonus_o[:, ln] = (_dot((r * k2 * rk_ref[:, ln]).astype(BF16), ones_bd) * v).astype(BF16)
        bv = kk * a

        cum = _split_dot(tri, log_w)
        w_inc = jnp.exp(cum)
        w_inv = jnp.exp(-cum)
        for c in range(w_o.shape[0]):
            w_o[c, :, ln] = w_inc[(c + 1) * CHUNK - 1:(c + 1) * CHUNK, :]
        at_o[:, ln] = (-kk * jnp.exp(cum - log_w)).astype(BF16)
        rt_o[:, ln] = (r * w_inc).astype(BF16)
        bt_o[:, ln] = (bv * w_inv).astype(BF16)
        kt_o[:, ln] = (k2 * w_inv).astype(BF16)
        v_o[:, ln] = v.astype(BF16)


def _prep(z, z_init, convw, mur, mul, w0, a0, k_k, k_a, r_k, wdec, aup, gup, ones_bd, tm):
    bsz, seq = z.shape[:2]
    g3 = 3 * GROUP_W
    lora_blk = 2 * g3 // LORA_PAD
    t_of = jnp.arange(tm)
    same_chunk = (t_of[:, None] // CHUNK) == (t_of[None, :] // CHUNK)
    tri = jnp.logical_and(same_chunk, t_of[None, :] <= t_of[:, None]).astype(BF16)

    halo_idx = lambda i: jnp.maximum(i * (tm // HALO) - 1, 0)
    row = lambda w, c: pl.BlockSpec((None, tm, w), lambda b, i: (b, i, c))
    halo = lambda w, c: pl.BlockSpec((None, HALO, w), lambda b, i: (b, halo_idx(i), c))
    init = lambda w, c: pl.BlockSpec((HALO, w), lambda b, i: (0, c))
    full = lambda a: pl.BlockSpec(a.shape, lambda b, i: (0,) * a.ndim)
    consts = (convw, mur, mul, w0, a0, k_k, k_a, r_k, wdec, aup, gup, ones_bd, tri)
    out_bf = jax.ShapeDtypeStruct((bsz, seq, GROUP_W), BF16)
    out_spec = pl.BlockSpec((None, tm, GROUP_W), lambda b, i: (b, i, 0))
    w_shape = jax.ShapeDtypeStruct((bsz, seq // CHUNK, 1, GROUP_W), F32)
    w_spec = pl.BlockSpec((None, tm // CHUNK, 1, GROUP_W), lambda b, i: (b, i, 0, 0))
    return pl.pallas_call(
        _prep_kernel,
        grid=(bsz, seq // tm),
        in_specs=[row(g3, 0), row(g3, 1), row(LORA_PAD, lora_blk),
                  halo(g3, 0), halo(g3, 1), halo(LORA_PAD, lora_blk),
                  init(g3, 0), init(g3, 1), init(LORA_PAD, lora_blk)] + [full(c) for c in consts],
        out_specs=[out_spec] * 6 + [w_spec] + [out_spec] * 2,
        out_shape=[out_bf] * 6 + [w_shape] + [out_bf] * 2,
        compiler_params=_params(("parallel", "arbitrary")),
        name="prep",
    )(z, z, z, z, z, z, z_init, z_init, z_init, *consts)


def _block_diag(x, mask):
    return jnp.where(mask, jnp.concatenate([x] * HEADS_PER_GROUP, axis=0), 0.0).astype(BF16)


def _scan_chain(chain_id, chunk, ready, rows, lanes, at_ref, rt_ref, bt_ref, kt_ref, v_ref, w_ref,
                bonus_ref, gate_ref, lnw_ref, lnb_ref, y_ref, state_ref):
    f = lambda ref: ref[rows, lanes].astype(F32)
    at, rt, bt, kt, v = map(f, (at_ref, rt_ref, bt_ref, kt_ref, v_ref))

    shape_bd = (SCAN_W, SCAN_W)
    bd_mask = (lax.broadcasted_iota(jnp.int32, shape_bd, 0) // HEAD_N
               == lax.broadcasted_iota(jnp.int32, shape_bd, 1) // HEAD_N)
    shape_c = (CHUNK, SCAN_W)
    t_idx = lax.broadcasted_iota(jnp.int32, shape_c, 0)
    s_idx = lax.broadcasted_iota(jnp.int32, shape_c, 1) % CHUNK
    bd = lambda x: _block_diag(x, bd_mask)
    stack = lambda a, b: jnp.concatenate([a, b], axis=0).astype(BF16)

    ar = stack(at, rt)
    ab = _dot_nt(ar, bd(bt))
    ak = _dot_nt(ar, bd(kt))
    yield
    a_ab = jnp.where(s_idx < t_idx, ab[:CHUNK], 0.0)
    a_rb = jnp.where(s_idx <= t_idx, ab[CHUNK:], 0.0)
    a_ak = jnp.where(s_idx < t_idx, ak[:CHUNK], 0.0)
    a_rk = jnp.where(s_idx <= t_idx, ak[CHUNK:], 0.0)

    inv = jnp.where(s_idx == t_idx, 1.0, 0.0) + a_ab
    pw = _dot(a_ab.astype(BF16), bd(a_ab))
    akv = _dot(stack(a_ak, a_rk), bd(v))
    yield
    for _ in range(int(math.log2(CHUNK)) - 2):
        both = _dot(stack(pw, inv), bd(pw))
        yield
        pw = both[:CHUNK]
        inv = inv + both[CHUNK:]
    inv = inv + _dot(inv.astype(BF16), bd(pw))
    assert chunk == 0 or (chain_id, chunk - 1) in ready
    state = state_ref[...]
    ars = _dot_nt(ar, state.astype(BF16))
    yield

    u = _dot(inv.astype(BF16), bd(ars[:CHUNK] + akv[:CHUNK]))
    yield
    y = ars[CHUNK:] + akv[CHUNK:] + _dot(a_rb.astype(BF16), bd(u))
    state_ref[...] = (state + jnp.where(bd_mask, _dot_tn(stack(u, v), stack(bt, kt)), 0.0)) * w_ref[chunk, :, lanes]
    ready.add((chain_id, chunk))
    yield

    ones_bd = jnp.where(bd_mask, 1.0, 0.0).astype(BF16)
    mean = _dot(y.astype(BF16), ones_bd) * (1.0 / HEAD_N)
    yield
    yc = y - mean
    var = _dot((yc * yc).astype(BF16), ones_bd) * (1.0 / HEAD_N)
    yield
    out = yc * lax.rsqrt(var + GN_EPS) * lnw_ref[:, lanes] + lnb_ref[:, lanes] + bonus_ref[rows, lanes]
    y_ref[rows, lanes] = (out * gate_ref[rows, lanes]).astype(BF16)


def _scan_kernel(*refs):
    seq_refs, (lnw_ref, lnb_ref, state0_ref, y_ref, state_out_ref, state_ref) = refs[:8], refs[8:]

    @pl.when(pl.program_id(0) == 0)
    def _():
        state_ref[...] = state0_ref[...]

    bsz, rows_per_step = y_ref.shape[:2]
    ready = set()

    def chunk_chains(c):
        rows = slice(c * CHUNK, (c + 1) * CHUNK)
        return [_scan_chain(b * N_GROUPS + g, c, ready, rows, slice(g * SCAN_W, (g + 1) * SCAN_W),
                            *[r.at[b] for r in seq_refs], lnw_ref, lnb_ref, y_ref.at[b],
                            state_ref.at[b * N_GROUPS + g])
                for b in range(bsz) for g in range(N_GROUPS)]

    pending = [chunk_chains(c) for c in range(rows_per_step // CHUNK)]
    chains, stage = [], 0
    while chains or pending:
        if pending and stage % SCAN_STAGGER == 0:
            chains = chains + pending.pop(0)
        chains = [c for c in chains if next(c, True) is None]
        stage += 1

    @pl.when(pl.program_id(0) == pl.num_programs(0) - 1)
    def _():
        state_out_ref[...] = state_ref[...]


def _scan(seq_arrays, lnw, lnb, state0):
    bsz, seq = seq_arrays[0].shape[:2]
    rows = SCAN_CHUNKS_PER_STEP * CHUNK if seq % (SCAN_CHUNKS_PER_STEP * CHUNK) == 0 else CHUNK
    blk = pl.BlockSpec((bsz, rows, GROUP_W), lambda c: (0, c, 0))
    wblk = pl.BlockSpec((bsz, rows // CHUNK, 1, GROUP_W), lambda c: (0, c, 0, 0))
    vec = pl.BlockSpec((1, GROUP_W), lambda c: (0, 0))
    st = pl.BlockSpec(state0.shape, lambda c: (0, 0, 0))
    return pl.pallas_call(
        _scan_kernel,
        grid=(seq // rows,),
        in_specs=[blk] * 5 + [wblk] + [blk] * 2 + [vec, vec, st],
        out_specs=[blk, st],
        out_shape=[jax.ShapeDtypeStruct((bsz, seq, GROUP_W), BF16), jax.ShapeDtypeStruct(state0.shape, F32)],
        scratch_shapes=[pltpu.VMEM(state0.shape, F32)],
        compiler_params=_params(("arbitrary",)),
        name="scan",
    )(*seq_arrays, lnw, lnb, state0)


def _outproj_kernel(yc_ref, yr_ref, h_ref, w_ref, gpost_ref, gpre_ref, h1_ref, n2_ref, *rest, cast):
    wb_ref = w_ref
    if cast:
        wb_ref = rest[0]
        wb_ref[...] = w_ref[...].astype(BF16)

    mix = _dot(yc_ref[...], wb_ref[:GROUP_W, :]) + _dot(yr_ref[...], wb_ref[GROUP_W:, :])
    h1 = h_ref[...] + _rms(mix, gpost_ref[...])
    h1_ref[...] = h1
    n2_ref[...] = _rms(h1, gpre_ref[...]).astype(BF16)


def _outproj(yc, yr, h2d, w, gpost, gpre, tm, cast):
    m = h2d.shape[0]
    assert not cast or m == tm
    row_spec = pl.BlockSpec((tm, D_MODEL), lambda i: (i, 0))
    w_spec = pl.BlockSpec((D_MODEL, D_MODEL), lambda i: (0, 0))
    out_shape = [jax.ShapeDtypeStruct((m, D_MODEL), F32), jax.ShapeDtypeStruct((m, D_MODEL), BF16)]
    return pl.pallas_call(
        functools.partial(_outproj_kernel, cast=cast),
        grid=(m // tm,),
        in_specs=[
            pl.BlockSpec((tm, GROUP_W), lambda i: (i, 0)),
            pl.BlockSpec((tm, GROUP_W), lambda i: (i, 0)),
            pl.BlockSpec((tm, D_MODEL), lambda i: (i, 0)),
            pl.BlockSpec((D_MODEL, D_MODEL), lambda i: (0, 0), pipeline_mode=pl.Buffered(1)),
            pl.BlockSpec((1, D_MODEL), lambda i: (0, 0)),
            pl.BlockSpec((1, D_MODEL), lambda i: (0, 0)),
        ],
        out_specs=[row_spec, row_spec] + ([w_spec] if cast else []),
        out_shape=out_shape + ([jax.ShapeDtypeStruct((D_MODEL, D_MODEL), BF16)] if cast else []),
        compiler_params=_params(("parallel",)),
        name="outproj",
    )(yc, yr, h2d, w, gpost, gpre)


def _ffn_up_kernel(n2_ref, wg_ref, wu_ref, cw_ref, n2pre_ref, wd_ref, act_ref, wdb_ref, wgb_ref, wub_ref,
                   carry0_ref, carry_ref):
    b, t = pl.program_id(1), pl.program_id(2)

    @pl.when(jnp.logical_and(b == 0, t == 0))
    def _():
        wgb_ref[...] = wg_ref[...].astype(BF16)
        wub_ref[...] = wu_ref[...].astype(BF16)
        wdb_ref[...] = wd_ref[...].astype(BF16)
        pre_gate = _dot(n2pre_ref[...], wgb_ref[...])
        carry0_ref[...] = pre_gate[pre_gate.shape[0] - HALO:, :]

    @pl.when(t == 0)
    def _():
        carry_ref[...] = carry0_ref[...]

    x = n2_ref[...]
    gate = _dot(x, wgb_ref[...])
    up = _dot(x, wub_ref[...])
    xs = jnp.concatenate([carry_ref[...], gate], axis=0)
    carry_ref[...] = gate[gate.shape[0] - HALO:, :]
    cw = cw_ref[...]
    gc = pltpu.roll(xs, 2, 0)[HALO:] * cw[0:1] + pltpu.roll(xs, 1, 0)[HALO:] * cw[1:2] + gate * cw[2:3]
    act_ref[...] = (gc * jax.nn.sigmoid(gc) * up).astype(BF16)


def _ffn_up(n2, w_gate_up, conv_ffn, n2_pre, w_down):
    bsz, seq = n2.shape[:2]
    n_tiles = D_FF // TN_UP
    return pl.pallas_call(
        _ffn_up_kernel,
        grid=(n_tiles, bsz, seq // TM_UP),
        in_specs=[
            pl.BlockSpec((None, TM_UP, D_MODEL), lambda j, b, t: (b, t, 0)),
            pl.BlockSpec((D_MODEL, TN_UP), lambda j, b, t: (0, j)),
            pl.BlockSpec((D_MODEL, TN_UP), lambda j, b, t: (0, j + n_tiles)),
            pl.BlockSpec((3, TN_UP), lambda j, b, t: (0, j)),
            pl.BlockSpec(n2_pre.shape, lambda j, b, t: (0, 0)),
            pl.BlockSpec((TN_UP, D_MODEL), lambda j, b, t: (j, 0)),
        ],
        out_specs=[pl.BlockSpec((None, TM_UP, TN_UP), lambda j, b, t: (b, t, j)),
                   pl.BlockSpec((TN_UP, D_MODEL), lambda j, b, t: (j, 0))],
        out_shape=[jax.ShapeDtypeStruct((bsz, seq, D_FF), BF16), jax.ShapeDtypeStruct((D_FF, D_MODEL), BF16)],
        scratch_shapes=[pltpu.VMEM((D_MODEL, TN_UP), BF16), pltpu.VMEM((D_MODEL, TN_UP), BF16),
                        pltpu.VMEM((HALO, TN_UP), F32), pltpu.VMEM((HALO, TN_UP), F32)],
        compiler_params=_params(("arbitrary", "arbitrary", "arbitrary")),
        name="ffn_up",
    )(n2, w_gate_up, w_gate_up, conv_ffn, n2_pre, w_down)


def _ffn_down_kernel(act_ref, w_ref, h1_ref, g_ref, out_ref):
    k = pl.program_id(1)

    @pl.when(k == 0)
    def _():
        out_ref[...] = _dot(act_ref[...], w_ref[...])

    @pl.when(k > 0)
    def _():
        out_ref[...] += _dot(act_ref[...], w_ref[...])

    @pl.when(k == pl.num_programs(1) - 1)
    def _():
        def body(c, carry):
            rows = pl.ds(pl.multiple_of(c * EPI_ROWS, EPI_ROWS), EPI_ROWS)
            out_ref[rows, :] = h1_ref[rows, :] + _rms(out_ref[rows, :], g_ref[...])
            return carry

        lax.fori_loop(0, out_ref.shape[0] // EPI_ROWS, body, 0)


def _ffn_down(act, w, h1, g):
    m = act.shape[0]
    tm = TM_DOWN
    return pl.pallas_call(
        _ffn_down_kernel,
        grid=(m // tm, D_FF // TK_DOWN),
        in_specs=[
            pl.BlockSpec((tm, TK_DOWN), lambda i, k: (i, k)),
            pl.BlockSpec((TK_DOWN, D_MODEL), lambda i, k: (k, 0)),
            pl.BlockSpec((tm, D_MODEL), lambda i, k: (i, 0)),
            pl.BlockSpec((1, D_MODEL), lambda i, k: (0, 0)),
        ],
        out_specs=pl.BlockSpec((tm, D_MODEL), lambda i, k: (i, 0)),
        out_shape=jax.ShapeDtypeStruct((m, D_MODEL), F32),
        compiler_params=_params(("parallel", "arbitrary")),
        name="ffn_down",
    )(act, w, h1, g)


def kernel(x, meta_tokens, g_pre_mix, w_in, conv_w, mu_rwkv, w_decay_up, w0, a_up, a0, g_up, k_k, k_a,
           r_k, ln_x_w, ln_x_b, w_out, g_post_mix, g_pre_ffn, w_gate_up, conv_ffn, w_down, g_post_ffn):
    bsz, seq, _ = x.shape
    assert x.shape[2] == D_MODEL and meta_tokens.shape == (N_META, D_MODEL)
    row = lambda p: p.reshape(1, -1).astype(F32)
    g3 = 3 * GROUP_W

    def window_rows(w, start, win):
        return jnp.pad(w, ((start - win[0], win[1] - start - w.shape[0]), (0, 0))).astype(BF16)

    mu_r, mu_l = row(mu_rwkv[:g3]), row(jnp.pad(mu_rwkv[g3:], (0, LORA_PAD - LORA_W)))
    wdec = window_rows(w_decay_up, 0, WD_WIN)
    aup = window_rows(a_up, DECAY_LORA, AD_WIN)
    gup = window_rows(g_up, DECAY_LORA + AAA_LORA, GD_WIN)
    head_of = jnp.arange(PREP_SLAB) // HEAD_N
    ones_bd = (head_of[:, None] == head_of[None, :]).astype(BF16)
    prep_consts = (conv_w.astype(F32), mu_r, mu_l, row(w0), row(a0), row(k_k), row(k_a), row(r_k),
                   wdec, aup, gup, ones_bd)
    gpre, gpost, gffn = row(g_pre_mix), row(g_post_mix), row(g_pre_ffn)
    lnw, lnb = row(ln_x_w), row(ln_x_b)

    pre = jnp.concatenate([jnp.zeros((PREFIX - N_META, D_MODEL), F32), meta_tokens.astype(F32)], axis=0)
    z_pre, w_in_b = _inproj(pre, gpre, w_in.T, PREFIX, TN_PROJ_PRE, cast=True)
    pre_out = _prep(z_pre[None], jnp.zeros((HALO, IN_COLS_PAD), F32), *prep_consts, PREFIX)
    y_pre, state_pre = _scan(pre_out[1:], lnw, lnb, jnp.zeros((N_GROUPS, SCAN_W, SCAN_W), F32))
    _, n2_pre, w_out_b = _outproj(pre_out[0][0], y_pre[0], pre, w_out, gpost, gffn, PREFIX, cast=True)

    m = bsz * seq
    x2d = x.reshape(m, D_MODEL)
    z = _inproj(x2d, gpre, w_in_b, TM_PROJ, TN_PROJ, cast=False).reshape(bsz, seq, IN_COLS_PAD)
    main_out = _prep(z, z_pre[PREFIX - HALO:], *prep_consts, TM_PREP)
    y_rwkv, _ = _scan(main_out[1:], lnw, lnb, jnp.tile(state_pre, (bsz, 1, 1)))
    h1, n2 = _outproj(main_out[0].reshape(m, GROUP_W), y_rwkv.reshape(m, GROUP_W), x2d, w_out_b, gpost, gffn,
                      TM_OUT, cast=False)
    act, w_down_b = _ffn_up(n2.reshape(bsz, seq, D_MODEL), w_gate_up, conv_ffn.astype(F32),
                            n2_pre[PREFIX - N_META:], w_down)
    out = _ffn_down(act.reshape(m, D_FF), w_down_b, h1, row(g_post_ffn))
    return out.reshape(bsz, seq, D_MODEL)
```

```python
import functools
import math

import jax
import jax.numpy as jnp
from jax import lax
from jax.experimental import pallas as pl
from jax.experimental.pallas import tpu as pltpu

F32 = jnp.float32
BF16 = jnp.bfloat16

D_MODEL = 2048
N_META = 16
GROUP_W = 1024
HEAD_N = 64
DECAY_LORA = 96
AAA_LORA = 96
GATE_LORA = 256
LANE = 128
LORA_W = DECAY_LORA + AAA_LORA + GATE_LORA
LORA_PAD = 512
WD_WIN = (0, LANE)
AD_WIN = (0, 2 * LANE)
GD_WIN = (LANE, LORA_PAD)
IN_COLS = 3 * GROUP_W + 3 * GROUP_W + LORA_W
IN_COLS_PAD = IN_COLS - LORA_W + LORA_PAD
D_FF = 5632
RMS_EPS = 1e-6
GN_EPS = 64e-5
DECAY_SCALE = math.exp(-0.5)

CHUNK = 64
assert CHUNK == HEAD_N
PREFIX = CHUNK
HALO = 8
HEADS_PER_GROUP = 4
SCAN_W = HEADS_PER_GROUP * HEAD_N
N_GROUPS = GROUP_W // SCAN_W
SCAN_CHUNKS_PER_STEP = 4
SCAN_STAGGER = 2

VMEM_LIMIT = 56 * 1024 * 1024

TM_PROJ = 1024
TN_PROJ = IN_COLS_PAD // 4
TN_PROJ_PRE = 512
TM_PREP = 256
PREP_SLAB = 256
TM_OUT = 512
TM_UP = 2048
UP_ROWS = 1024
TN_UP = 512
TM_DOWN = 1024
TK_DOWN = D_FF // 4
EPI_ROWS = 128


def _dot(a, b):
    return jnp.dot(a, b, preferred_element_type=F32)


def _dot_nt(a, b):
    return lax.dot_general(a, b, (((1,), (1,)), ((), ())), preferred_element_type=F32)


def _dot_tn(a, b):
    return lax.dot_general(a, b, (((0,), (0,)), ((), ())), preferred_element_type=F32)


def _rms(x, g):
    return x * lax.rsqrt(jnp.mean(x * x, axis=-1, keepdims=True) + RMS_EPS) * g


def _params(sem):
    return pltpu.CompilerParams(dimension_semantics=sem, vmem_limit_bytes=VMEM_LIMIT)


def _inproj_kernel(h_ref, g_ref, w_ref, z_ref, *rest, cast):
    xn_ref = rest[-1]

    @pl.when(pl.program_id(1) == 0)
    def _():
        xn_ref[...] = _rms(h_ref[...], g_ref[...]).astype(BF16)

    if cast:
        tn = w_ref.shape[0]
        col = pl.program_id(1) * tn + lax.broadcasted_iota(jnp.int32, (tn, 1), 0)
        rest[0][...] = jnp.where(col < IN_COLS, w_ref[...], 0.0).T.astype(BF16)
        z_ref[...] = _dot(xn_ref[...], rest[0][...])
    else:
        z_ref[...] = _dot(xn_ref[...], w_ref[...])


def _inproj(h2d, g, w, tm, tn, cast):
    m = h2d.shape[0]
    z_spec = pl.BlockSpec((tm, tn), lambda i, j: (i, j))
    z_shape = jax.ShapeDtypeStruct((m, IN_COLS_PAD), F32)
    w_spec = pl.BlockSpec((D_MODEL, tn), lambda i, j: (0, j))
    return pl.pallas_call(
        functools.partial(_inproj_kernel, cast=cast),
        grid=(m // tm, IN_COLS_PAD // tn),
        in_specs=[
            pl.BlockSpec((tm, D_MODEL), lambda i, j: (i, 0)),
            pl.BlockSpec((1, D_MODEL), lambda i, j: (0, 0)),
            pl.BlockSpec((tn, D_MODEL), lambda i, j: (j, 0)) if cast else w_spec,
        ],
        out_specs=[z_spec, w_spec] if cast else z_spec,
        out_shape=[z_shape, jax.ShapeDtypeStruct((D_MODEL, IN_COLS_PAD), BF16)] if cast else z_shape,
        scratch_shapes=[pltpu.VMEM((tm, D_MODEL), BF16)],
        compiler_params=_params(("parallel", "arbitrary")),
        name="inproj",
    )(h2d, g, w)


def _prev_rows(cur, halo, shift):
    xs = jnp.concatenate([halo, cur], axis=0)
    return pltpu.roll(xs, shift, 0)[HALO:]


def _split_dot(m01, x):
    x1 = x.astype(BF16)
    r1 = x - x1.astype(F32)
    x2 = r1.astype(BF16)
    x3 = (r1 - x2.astype(F32)).astype(BF16)
    return _dot(m01, x1) + _dot(m01, x2) + _dot(m01, x3)


def _prep_kernel(zc_ref, zr_ref, zl_ref, hc_ref, hr_ref, hl_ref, ic_ref, ir_ref, il_ref,
                 convw_ref, mur_ref, mul_ref, w0_ref, a0_ref, kk_ref, ka_ref, rk_ref,
                 wdec_ref, aup_ref, gup_ref, ones_ref, tri_ref,
                 yconv_o, at_o, rt_o, bt_o, kt_o, v_o, w_o, bonus_o, gate_o):
    first = pl.program_id(1) == 0
    g, w = GROUP_W, PREP_SLAB
    halo_of = lambda init_ref, halo_ref, c: jnp.where(first, init_ref[:, c], halo_ref[:, c])

    zl = zl_ref[...]
    zsl = zl + (_prev_rows(zl, halo_of(il_ref, hl_ref, slice(None)), 1) - zl) * mul_ref[...]
    wd = jnp.tanh(zsl[:, WD_WIN[0]:WD_WIN[1]]).astype(BF16)
    ad = zsl[:, AD_WIN[0]:AD_WIN[1]].astype(BF16)
    gd = jax.nn.sigmoid(zsl[:, GD_WIN[0]:GD_WIN[1]]).astype(BF16)
    ones_bd = ones_ref[...]
    tri = tri_ref[...]
    cw = convw_ref[...]

    for s in range(g // w):
        ln = slice(s * w, (s + 1) * w)
        col = lambda sec: slice(sec * g + s * w, sec * g + (s + 1) * w)

        p = zc_ref[:, col(1)] * zc_ref[:, col(2)]
        ph = halo_of(ic_ref, hc_ref, col(1)) * halo_of(ic_ref, hc_ref, col(2))
        conv = _prev_rows(p, ph, 2) * cw[0:1, ln] + _prev_rows(p, ph, 1) * cw[1:2, ln] + p * cw[2:3, ln]
        yconv_o[:, ln] = (zc_ref[:, col(0)] * conv).astype(BF16)

        def shifted(sec):
            z = zr_ref[:, col(sec)]
            return z + (_prev_rows(z, halo_of(ir_ref, hr_ref, col(sec)), 1) - z) * mur_ref[:, col(sec)]

        r, k, v = shifted(0), shifted(1), shifted(2)
        log_w = -DECAY_SCALE * jax.nn.sigmoid(w0_ref[:, ln] + _dot(wd, wdec_ref[:, ln]))
        a = jax.nn.sigmoid(a0_ref[:, ln] + _dot(ad, aup_ref[:, ln]))
        gate_o[:, ln] = _dot(gd, gup_ref[:, ln]).astype(BF16)

        kk = k * kk_ref[:, ln]
        kk = kk * lax.rsqrt(jnp.maximum(_dot((kk * kk).astype(BF16), ones_bd), 1e-24))
        k2 = k * (1.0 + (a - 1.0) * ka_ref[:, ln])
        bonus_o[:, ln] = (_dot((r * k2 * rk_ref[:, ln]).astype(BF16), ones_bd) * v).astype(BF16)
        bv = kk * a

        cum = _split_dot(tri, log_w)
        w_inc = jnp.exp(cum)
        w_inv = jnp.exp(-cum)
        for c in range(w_o.shape[0]):
            w_o[c, :, ln] = w_inc[(c + 1) * CHUNK - 1:(c + 1) * CHUNK, :]
        at_o[:, ln] = (-kk * jnp.exp(cum - log_w)).astype(BF16)
        rt_o[:, ln] = (r * w_inc).astype(BF16)
        bt_o[:, ln] = (bv * w_inv).astype(BF16)
        kt_o[:, ln] = (k2 * w_inv).astype(BF16)
        v_o[:, ln] = v.astype(BF16)


def _prep(z, z_init, convw, mur, mul, w0, a0, k_k, k_a, r_k, wdec, aup, gup, ones_bd, tm):
    bsz, seq = z.shape[:2]
    g3 = 3 * GROUP_W
    lora_blk = 2 * g3 // LORA_PAD
    t_of = jnp.arange(tm)
    same_chunk = (t_of[:, None] // CHUNK) == (t_of[None, :] // CHUNK)
    tri = jnp.logical_and(same_chunk, t_of[None, :] <= t_of[:, None]).astype(BF16)

    halo_idx = lambda i: jnp.maximum(i * (tm // HALO) - 1, 0)
    row = lambda w, c: pl.BlockSpec((None, tm, w), lambda b, i: (b, i, c))
    halo = lambda w, c: pl.BlockSpec((None, HALO, w), lambda b, i: (b, halo_idx(i), c))
    init = lambda w, c: pl.BlockSpec((HALO, w), lambda b, i: (0, c))
    full = lambda a: pl.BlockSpec(a.shape, lambda b, i: (0,) * a.ndim)
    consts = (convw, mur, mul, w0, a0, k_k, k_a, r_k, wdec, aup, gup, ones_bd, tri)
    out_bf = jax.ShapeDtypeStruct((bsz, seq, GROUP_W), BF16)
    out_spec = pl.BlockSpec((None, tm, GROUP_W), lambda b, i: (b, i, 0))
    w_shape = jax.ShapeDtypeStruct((bsz, seq // CHUNK, 1, GROUP_W), F32)
    w_spec = pl.BlockSpec((None, tm // CHUNK, 1, GROUP_W), lambda b, i: (b, i, 0, 0))
    return pl.pallas_call(
        _prep_kernel,
        grid=(bsz, seq // tm),
        in_specs=[row(g3, 0), row(g3, 1), row(LORA_PAD, lora_blk),
                  halo(g3, 0), halo(g3, 1), halo(LORA_PAD, lora_blk),
                  init(g3, 0), init(g3, 1), init(LORA_PAD, lora_blk)] + [full(c) for c in consts],
        out_specs=[out_spec] * 6 + [w_spec] + [out_spec] * 2,
        out_shape=[out_bf] * 6 + [w_shape] + [out_bf] * 2,
        compiler_params=_params(("parallel", "arbitrary")),
        name="prep",
    )(z, z, z, z, z, z, z_init, z_init, z_init, *consts)


def _block_diag(x, mask):
    return jnp.where(mask, jnp.concatenate([x] * HEADS_PER_GROUP, axis=0), 0.0).astype(BF16)


def _scan_chain(chain_id, chunk, ready, rows, lanes, at_ref, rt_ref, bt_ref, kt_ref, v_ref, w_ref,
                bonus_ref, gate_ref, lnw_ref, lnb_ref, y_ref, state_ref):
    f = lambda ref: ref[rows, lanes].astype(F32)
    at, rt, bt, kt, v = map(f, (at_ref, rt_ref, bt_ref, kt_ref, v_ref))

    shape_bd = (SCAN_W, SCAN_W)
    bd_mask = (lax.broadcasted_iota(jnp.int32, shape_bd, 0) // HEAD_N
               == lax.broadcasted_iota(jnp.int32, shape_bd, 1) // HEAD_N)
    shape_c = (CHUNK, SCAN_W)
    t_idx = lax.broadcasted_iota(jnp.int32, shape_c, 0)
    s_idx = lax.broadcasted_iota(jnp.int32, shape_c, 1) % CHUNK
    bd = lambda x: _block_diag(x, bd_mask)
    stack = lambda a, b: jnp.concatenate([a, b], axis=0).astype(BF16)

    ar = stack(at, rt)
    ab = _dot_nt(ar, bd(bt))
    ak = _dot_nt(ar, bd(kt))
    yield
    a_ab = jnp.where(s_idx < t_idx, ab[:CHUNK], 0.0)
    a_rb = jnp.where(s_idx <= t_idx, ab[CHUNK:], 0.0)
    a_ak = jnp.where(s_idx < t_idx, ak[:CHUNK], 0.0)
    a_rk = jnp.where(s_idx <= t_idx, ak[CHUNK:], 0.0)

    inv = jnp.where(s_idx == t_idx, 1.0, 0.0) + a_ab
    pw = _dot(a_ab.astype(BF16), bd(a_ab))
    akv = _dot(stack(a_ak, a_rk), bd(v))
    yield
    for _ in range(int(math.log2(CHUNK)) - 2):
        both = _dot(stack(pw, inv), bd(pw))
        yield
        pw = both[:CHUNK]
        inv = inv + both[CHUNK:]
    inv = inv + _dot(inv.astype(BF16), bd(pw))
    assert chunk == 0 or (chain_id, chunk - 1) in ready
    state = state_ref[...]
    ars = _dot_nt(ar, state.astype(BF16))
    yield

    u = _dot(inv.astype(BF16), bd(ars[:CHUNK] + akv[:CHUNK]))
    yield
    y = ars[CHUNK:] + akv[CHUNK:] + _dot(a_rb.astype(BF16), bd(u))
    state_ref[...] = (state + jnp.where(bd_mask, _dot_tn(stack(u, v), stack(bt, kt)), 0.0)) * w_ref[chunk, :, lanes]
    ready.add((chain_id, chunk))
    yield

    ones_bd = jnp.where(bd_mask, 1.0, 0.0).astype(BF16)
    mean = _dot(y.astype(BF16), ones_bd) * (1.0 / HEAD_N)
    yield
    yc = y - mean
    var = _dot((yc * yc).astype(BF16), ones_bd) * (1.0 / HEAD_N)
    yield
    out = yc * lax.rsqrt(var + GN_EPS) * lnw_ref[:, lanes] + lnb_ref[:, lanes] + bonus_ref[rows, lanes]
    y_ref[rows, lanes] = (out * gate_ref[rows, lanes]).astype(BF16)


def _scan_kernel(*refs):
    seq_refs, (lnw_ref, lnb_ref, state0_ref, y_ref, state_out_ref, state_ref) = refs[:8], refs[8:]

    @pl.when(pl.program_id(0) == 0)
    def _():
        state_ref[...] = state0_ref[...]

    bsz, rows_per_step = y_ref.shape[:2]
    ready = set()

    def chunk_chains(c):
        rows = slice(c * CHUNK, (c + 1) * CHUNK)
        return [_scan_chain(b * N_GROUPS + g, c, ready, rows, slice(g * SCAN_W, (g + 1) * SCAN_W),
                            *[r.at[b] for r in seq_refs], lnw_ref, lnb_ref, y_ref.at[b],
                            state_ref.at[b * N_GROUPS + g])
                for b in range(bsz) for g in range(N_GROUPS)]

    pending = [chunk_chains(c) for c in range(rows_per_step // CHUNK)]
    chains, stage = [], 0
    while chains or pending:
        if pending and stage % SCAN_STAGGER == 0:
            chains = chains + pending.pop(0)
        chains = [c for c in chains if next(c, True) is None]
        stage += 1

    @pl.when(pl.program_id(0) == pl.num_programs(0) - 1)
    def _():
        state_out_ref[...] = state_ref[...]


def _scan(seq_arrays, lnw, lnb, state0):
    bsz, seq = seq_arrays[0].shape[:2]
    rows = SCAN_CHUNKS_PER_STEP * CHUNK if seq % (SCAN_CHUNKS_PER_STEP * CHUNK) == 0 else CHUNK
    blk = pl.BlockSpec((bsz, rows, GROUP_W), lambda c: (0, c, 0))
    wblk = pl.BlockSpec((bsz, rows // CHUNK, 1, GROUP_W), lambda c: (0, c, 0, 0))
    vec = pl.BlockSpec((1, GROUP_W), lambda c: (0, 0))
    st = pl.BlockSpec(state0.shape, lambda c: (0, 0, 0))
    return pl.pallas_call(
        _scan_kernel,
        grid=(seq // rows,),
        in_specs=[blk] * 5 + [wblk] + [blk] * 2 + [vec, vec, st],
        out_specs=[blk, st],
        out_shape=[jax.ShapeDtypeStruct((bsz, seq, GROUP_W), BF16), jax.ShapeDtypeStruct(state0.shape, F32)],
        scratch_shapes=[pltpu.VMEM(state0.shape, F32)],
        compiler_params=_params(("arbitrary",)),
        name="scan",
    )(*seq_arrays, lnw, lnb, state0)


def _outproj_kernel(yc_ref, yr_ref, h_ref, w_ref, gpost_ref, gpre_ref, *rest, cast):
    if cast:
        h1_ref, n2_ref, wb_ref = rest
        wb_ref[...] = w_ref[...].astype(BF16)
    else:
        wd_ref, h1_ref, n2_ref, wdb_ref = rest
        wb_ref = w_ref
        wdb_ref[...] = wd_ref[...].astype(BF16)

    mix = _dot(yc_ref[...], wb_ref[:GROUP_W, :]) + _dot(yr_ref[...], wb_ref[GROUP_W:, :])
    h1 = h_ref[...] + _rms(mix, gpost_ref[...])
    h1_ref[...] = h1
    n2_ref[...] = _rms(h1, gpre_ref[...]).astype(BF16)


def _outproj(yc, yr, h2d, w, gpost, gpre, tm, w_down=None):
    cast = w_down is None
    m = h2d.shape[0]
    n = m // tm
    assert not cast or n == 1
    row_spec = pl.BlockSpec((tm, D_MODEL), lambda i: (i, 0))
    out_shape = [jax.ShapeDtypeStruct((m, D_MODEL), F32), jax.ShapeDtypeStruct((m, D_MODEL), BF16)]
    if cast:
        extra_in, extra_args = [], ()
        extra_out = pl.BlockSpec((D_MODEL, D_MODEL), lambda i: (0, 0))
        extra_shape = jax.ShapeDtypeStruct((D_MODEL, D_MODEL), BF16)
    else:
        slab = pl.BlockSpec((D_FF // n, D_MODEL), lambda i: (i, 0))
        extra_in, extra_args = [slab], (w_down,)
        extra_out, extra_shape = slab, jax.ShapeDtypeStruct((D_FF, D_MODEL), BF16)
    return pl.pallas_call(
        functools.partial(_outproj_kernel, cast=cast),
        grid=(m // tm,),
        in_specs=[
            pl.BlockSpec((tm, GROUP_W), lambda i: (i, 0)),
            pl.BlockSpec((tm, GROUP_W), lambda i: (i, 0)),
            pl.BlockSpec((tm, D_MODEL), lambda i: (i, 0)),
            pl.BlockSpec((D_MODEL, D_MODEL), lambda i: (0, 0), pipeline_mode=pl.Buffered(1)),
            pl.BlockSpec((1, D_MODEL), lambda i: (0, 0)),
            pl.BlockSpec((1, D_MODEL), lambda i: (0, 0)),
        ] + extra_in,
        out_specs=[row_spec, row_spec, extra_out],
        out_shape=out_shape + [extra_shape],
        compiler_params=_params(("parallel",)),
        name="outproj",
    )(yc, yr, h2d, w, gpost, gpre, *extra_args)


def _ffn_up_kernel(n2_ref, wg_ref, wu_ref, cw_ref, n2pre_ref, act_ref, wgb_ref, wub_ref, carry0_ref, carry_ref):
    b, t = pl.program_id(1), pl.program_id(2)

    @pl.when(jnp.logical_and(b == 0, t == 0))
    def _():
        wgb_ref[...] = wg_ref[...].astype(BF16)
        wub_ref[...] = wu_ref[...].astype(BF16)
        pre_gate = _dot(n2pre_ref[...], wgb_ref[...])
        carry0_ref[...] = pre_gate[pre_gate.shape[0] - HALO:, :]

    @pl.when(t == 0)
    def _():
        carry_ref[...] = carry0_ref[...]

    cw = cw_ref[...]
    for r0 in range(0, n2_ref.shape[0], UP_ROWS):
        rows = slice(r0, r0 + UP_ROWS)
        x = n2_ref[rows, :]
        gate = _dot(x, wgb_ref[...])
        up = _dot(x, wub_ref[...])
        xs = jnp.concatenate([carry_ref[...], gate], axis=0)
        carry_ref[...] = gate[UP_ROWS - HALO:, :]
        gc = pltpu.roll(xs, 2, 0)[HALO:] * cw[0:1] + pltpu.roll(xs, 1, 0)[HALO:] * cw[1:2] + gate * cw[2:3]
        act_ref[rows, :] = (gc * jax.nn.sigmoid(gc) * up).astype(BF16)


def _ffn_up(n2, w_gate_up, conv_ffn, n2_pre):
    bsz, seq = n2.shape[:2]
    n_tiles = D_FF // TN_UP
    return pl.pallas_call(
        _ffn_up_kernel,
        grid=(n_tiles, bsz, seq // TM_UP),
        in_specs=[
            pl.BlockSpec((None, TM_UP, D_MODEL), lambda j, b, t: (b, t, 0)),
            pl.BlockSpec((D_MODEL, TN_UP), lambda j, b, t: (0, j)),
            pl.BlockSpec((D_MODEL, TN_UP), lambda j, b, t: (0, j + n_tiles)),
            pl.BlockSpec((3, TN_UP), lambda j, b, t: (0, j)),
            pl.BlockSpec(n2_pre.shape, lambda j, b, t: (0, 0)),
        ],
        out_specs=pl.BlockSpec((None, TM_UP, TN_UP), lambda j, b, t: (b, t, j)),
        out_shape=jax.ShapeDtypeStruct((bsz, seq, D_FF), BF16),
        scratch_shapes=[pltpu.VMEM((D_MODEL, TN_UP), BF16), pltpu.VMEM((D_MODEL, TN_UP), BF16),
                        pltpu.VMEM((HALO, TN_UP), F32), pltpu.VMEM((HALO, TN_UP), F32)],
        compiler_params=_params(("arbitrary", "arbitrary", "arbitrary")),
        name="ffn_up",
    )(n2, w_gate_up, w_gate_up, conv_ffn, n2_pre)


def _ffn_down_kernel(act_ref, w_ref, h1_ref, g_ref, out_ref):
    k = pl.program_id(1)

    @pl.when(k == 0)
    def _():
        out_ref[...] = _dot(act_ref[...], w_ref[...])

    @pl.when(k > 0)
    def _():
        out_ref[...] += _dot(act_ref[...], w_ref[...])

    @pl.when(k == pl.num_programs(1) - 1)
    def _():
        def body(c, carry):
            rows = pl.ds(pl.multiple_of(c * EPI_ROWS, EPI_ROWS), EPI_ROWS)
            out_ref[rows, :] = h1_ref[rows, :] + _rms(out_ref[rows, :], g_ref[...])
            return carry

        lax.fori_loop(0, out_ref.shape[0] // EPI_ROWS, body, 0)


def _ffn_down(act, w, h1, g):
    m = act.shape[0]
    tm = TM_DOWN
    return pl.pallas_call(
        _ffn_down_kernel,
        grid=(m // tm, D_FF // TK_DOWN),
        in_specs=[
            pl.BlockSpec((tm, TK_DOWN), lambda i, k: (i, k)),
            pl.BlockSpec((TK_DOWN, D_MODEL), lambda i, k: (k, 0)),
            pl.BlockSpec((tm, D_MODEL), lambda i, k: (i, 0)),
            pl.BlockSpec((1, D_MODEL), lambda i, k: (0, 0)),
        ],
        out_specs=pl.BlockSpec((tm, D_MODEL), lambda i, k: (i, 0)),
        out_shape=jax.ShapeDtypeStruct((m, D_MODEL), F32),
        compiler_params=_params(("parallel", "arbitrary")),
        name="ffn_down",
    )(act, w, h1, g)


def kernel(x, meta_tokens, g_pre_mix, w_in, conv_w, mu_rwkv, w_decay_up, w0, a_up, a0, g_up, k_k, k_a,
           r_k, ln_x_w, ln_x_b, w_out, g_post_mix, g_pre_ffn, w_gate_up, conv_ffn, w_down, g_post_ffn):
    bsz, seq, _ = x.shape
    assert x.shape[2] == D_MODEL and meta_tokens.shape == (N_META, D_MODEL)
    row = lambda p: p.reshape(1, -1).astype(F32)
    g3 = 3 * GROUP_W

    def window_rows(w, start, win):
        return jnp.pad(w, ((start - win[0], win[1] - start - w.shape[0]), (0, 0))).astype(BF16)

    mu_r, mu_l = row(mu_rwkv[:g3]), row(jnp.pad(mu_rwkv[g3:], (0, LORA_PAD - LORA_W)))
    wdec = window_rows(w_decay_up, 0, WD_WIN)
    aup = window_rows(a_up, DECAY_LORA, AD_WIN)
    gup = window_rows(g_up, DECAY_LORA + AAA_LORA, GD_WIN)
    head_of = jnp.arange(PREP_SLAB) // HEAD_N
    ones_bd = (head_of[:, None] == head_of[None, :]).astype(BF16)
    prep_consts = (conv_w.astype(F32), mu_r, mu_l, row(w0), row(a0), row(k_k), row(k_a), row(r_k),
                   wdec, aup, gup, ones_bd)
    gpre, gpost, gffn = row(g_pre_mix), row(g_post_mix), row(g_pre_ffn)
    lnw, lnb = row(ln_x_w), row(ln_x_b)

    pre = jnp.concatenate([jnp.zeros((PREFIX - N_META, D_MODEL), F32), meta_tokens.astype(F32)], axis=0)
    z_pre, w_in_b = _inproj(pre, gpre, w_in.T, PREFIX, TN_PROJ_PRE, cast=True)
    pre_out = _prep(z_pre[None], jnp.zeros((HALO, IN_COLS_PAD), F32), *prep_consts, PREFIX)
    y_pre, state_pre = _scan(pre_out[1:], lnw, lnb, jnp.zeros((N_GROUPS, SCAN_W, SCAN_W), F32))
    _, n2_pre, w_out_b = _outproj(pre_out[0][0], y_pre[0], pre, w_out, gpost, gffn, PREFIX)

    m = bsz * seq
    x2d = x.reshape(m, D_MODEL)
    z = _inproj(x2d, gpre, w_in_b, TM_PROJ, TN_PROJ, cast=False).reshape(bsz, seq, IN_COLS_PAD)
    main_out = _prep(z, z_pre[PREFIX - HALO:], *prep_consts, TM_PREP)
    y_rwkv, _ = _scan(main_out[1:], lnw, lnb, jnp.tile(state_pre, (bsz, 1, 1)))
    h1, n2, w_down_b = _outproj(main_out[0].reshape(m, GROUP_W), y_rwkv.reshape(m, GROUP_W), x2d, w_out_b, gpost,
                                gffn, TM_OUT, w_down)
    act = _ffn_up(n2.reshape(bsz, seq, D_MODEL), w_gate_up, conv_ffn.astype(F32), n2_pre[PREFIX - N_META:])
    out = _ffn_down(act.reshape(m, D_FF), w_down_b, h1, row(g_post_ffn))
    return out.reshape(bsz, seq, D_MODEL)
```

```python
import functools
import math

import jax
import jax.numpy as jnp
from jax import lax
from jax.experimental import pallas as pl
from jax.experimental.pallas import tpu as pltpu

F32 = jnp.float32
BF16 = jnp.bfloat16

D_MODEL = 2048
N_META = 16
GROUP_W = 1024
HEAD_N = 64
DECAY_LORA = 96
AAA_LORA = 96
GATE_LORA = 256
LANE = 128
LORA_W = DECAY_LORA + AAA_LORA + GATE_LORA
LORA_PAD = 512
WD_WIN = (0, LANE)
AD_WIN = (0, 2 * LANE)
GD_WIN = (LANE, LORA_PAD)
IN_COLS = 3 * GROUP_W + 3 * GROUP_W + LORA_W
IN_COLS_PAD = IN_COLS - LORA_W + LORA_PAD
D_FF = 5632
RMS_EPS = 1e-6
GN_EPS = 64e-5
DECAY_SCALE = math.exp(-0.5)

CHUNK = 64
assert CHUNK == HEAD_N
PREFIX = CHUNK
HALO = 8
HEADS_PER_GROUP = 4
SCAN_W = HEADS_PER_GROUP * HEAD_N
N_GROUPS = GROUP_W // SCAN_W
SCAN_CHUNKS_PER_STEP = 4
SCAN_STAGGER = 2

VMEM_LIMIT = 56 * 1024 * 1024

TM_PROJ = 1024
TN_PROJ = IN_COLS_PAD // 4
TN_PROJ_PRE = 512
TM_PREP = 256
PREP_SLAB = 256
TM_OUT = 512
OUT_ROWS = 128
TM_UP = 2048
UP_ROWS = 1024
TN_UP = 512
TM_DOWN = 1024
TK_DOWN = D_FF // 4
DOWN_ROWS = 256
assert D_FF // TK_DOWN > 1
PROJ_ROWS = 256


def _dot(a, b):
    return jnp.dot(a, b, preferred_element_type=F32)


def _dot_nt(a, b):
    return lax.dot_general(a, b, (((1,), (1,)), ((), ())), preferred_element_type=F32)


def _dot_tn(a, b):
    return lax.dot_general(a, b, (((0,), (0,)), ((), ())), preferred_element_type=F32)


def _rms(x, g):
    return x * lax.rsqrt(jnp.mean(x * x, axis=-1, keepdims=True) + RMS_EPS) * g


def _params(sem):
    return pltpu.CompilerParams(dimension_semantics=sem, vmem_limit_bytes=VMEM_LIMIT)


def _inproj_kernel(h_ref, g_ref, w_ref, z_ref, *rest, cast):
    xn_ref = rest[-1]
    j = pl.program_id(1)

    if cast:
        @pl.when(j == 0)
        def _():
            xn_ref[...] = _rms(h_ref[...], g_ref[...]).astype(BF16)

        tn = w_ref.shape[0]
        col = j * tn + lax.broadcasted_iota(jnp.int32, (tn, 1), 0)
        rest[0][...] = jnp.where(col < IN_COLS, w_ref[...], 0.0).T.astype(BF16)
        z_ref[...] = _dot(xn_ref[...], rest[0][...])
        return

    @pl.when(j == 0)
    def _():
        for r0 in range(0, h_ref.shape[0], PROJ_ROWS):
            rows = slice(r0, r0 + PROJ_ROWS)
            xn_ref[rows, :] = _rms(h_ref[rows, :], g_ref[...]).astype(BF16)
            z_ref[rows, :] = _dot(xn_ref[rows, :], w_ref[...])

    @pl.when(j > 0)
    def _():
        z_ref[...] = _dot(xn_ref[...], w_ref[...])


def _inproj(h2d, g, w, tm, tn, cast):
    m = h2d.shape[0]
    z_spec = pl.BlockSpec((tm, tn), lambda i, j: (i, j))
    z_shape = jax.ShapeDtypeStruct((m, IN_COLS_PAD), F32)
    w_spec = pl.BlockSpec((D_MODEL, tn), lambda i, j: (0, j))
    return pl.pallas_call(
        functools.partial(_inproj_kernel, cast=cast),
        grid=(m // tm, IN_COLS_PAD // tn),
        in_specs=[
            pl.BlockSpec((tm, D_MODEL), lambda i, j: (i, 0)),
            pl.BlockSpec((1, D_MODEL), lambda i, j: (0, 0)),
            pl.BlockSpec((tn, D_MODEL), lambda i, j: (j, 0)) if cast else w_spec,
        ],
        out_specs=[z_spec, w_spec] if cast else z_spec,
        out_shape=[z_shape, jax.ShapeDtypeStruct((D_MODEL, IN_COLS_PAD), BF16)] if cast else z_shape,
        scratch_shapes=[pltpu.VMEM((tm, D_MODEL), BF16)],
        compiler_params=_params(("parallel", "arbitrary")),
        name="inproj",
    )(h2d, g, w)


def _prev_rows(cur, halo, shift):
    xs = jnp.concatenate([halo, cur], axis=0)
    return pltpu.roll(xs, shift, 0)[HALO:]


def _split_dot(m01, x):
    x1 = x.astype(BF16)
    r1 = x - x1.astype(F32)
    x2 = r1.astype(BF16)
    x3 = (r1 - x2.astype(F32)).astype(BF16)
    return _dot(m01, x1) + _dot(m01, x2) + _dot(m01, x3)


def _prep_kernel(zc_ref, zr_ref, zl_ref, hc_ref, hr_ref, hl_ref, ic_ref, ir_ref, il_ref,
                 convw_ref, mur_ref, mul_ref, w0_ref, a0_ref, kk_ref, ka_ref, rk_ref,
                 wdec_ref, aup_ref, gup_ref, ones_ref, tri_ref,
                 yconv_o, at_o, rt_o, bt_o, kt_o, v_o, w_o, bonus_o, gate_o):
    first = pl.program_id(1) == 0
    g, w = GROUP_W, PREP_SLAB
    halo_of = lambda init_ref, halo_ref, c: jnp.where(first, init_ref[:, c], halo_ref[:, c])

    zl = zl_ref[...]
    zsl = zl + (_prev_rows(zl, halo_of(il_ref, hl_ref, slice(None)), 1) - zl) * mul_ref[...]
    wd = jnp.tanh(zsl[:, WD_WIN[0]:WD_WIN[1]]).astype(BF16)
    ad = zsl[:, AD_WIN[0]:AD_WIN[1]].astype(BF16)
    gd = jax.nn.sigmoid(zsl[:, GD_WIN[0]:GD_WIN[1]]).astype(BF16)
    ones_bd = ones_ref[...]
    tri = tri_ref[...]
    cw = convw_ref[...]

    for s in range(g // w):
        ln = slice(s * w, (s + 1) * w)
        col = lambda sec: slice(sec * g + s * w, sec * g + (s + 1) * w)

        p = zc_ref[:, col(1)] * zc_ref[:, col(2)]
        ph = halo_of(ic_ref, hc_ref, col(1)) * halo_of(ic_ref, hc_ref, col(2))
        conv = _prev_rows(p, ph, 2) * cw[0:1, ln] + _prev_rows(p, ph, 1) * cw[1:2, ln] + p * cw[2:3, ln]
        yconv_o[:, ln] = (zc_ref[:, col(0)] * conv).astype(BF16)

        def shifted(sec):
            z = zr_ref[:, col(sec)]
            return z + (_prev_rows(z, halo_of(ir_ref, hr_ref, col(sec)), 1) - z) * mur_ref[:, col(sec)]

        r, k, v = shifted(0), shifted(1), shifted(2)
        log_w = -DECAY_SCALE * jax.nn.sigmoid(w0_ref[:, ln] + _dot(wd, wdec_ref[:, ln]))
        a = jax.nn.sigmoid(a0_ref[:, ln] + _dot(ad, aup_ref[:, ln]))
        gate_o[:, ln] = _dot(gd, gup_ref[:, ln]).astype(BF16)

        kk = k * kk_ref[:, ln]
        kk = kk * lax.rsqrt(jnp.maximum(_dot((kk * kk).astype(BF16), ones_bd), 1e-24))
        k2 = k * (1.0 + (a - 1.0) * ka_ref[:, ln])
        bonus_o[:, ln] = (_dot((r * k2 * rk_ref[:, ln]).astype(BF16), ones_bd) * v).astype(BF16)
        bv = kk * a

        cum = _split_dot(tri, log_w)
        w_inc = jnp.exp(cum)
        w_inv = jnp.exp(-cum)
        for c in range(w_o.shape[0]):
            w_o[c, :, ln] = w_inc[(c + 1) * CHUNK - 1:(c + 1) * CHUNK, :]
        at_o[:, ln] = (-kk * jnp.exp(cum - log_w)).astype(BF16)
        rt_o[:, ln] = (r * w_inc).astype(BF16)
        bt_o[:, ln] = (bv * w_inv).astype(BF16)
        kt_o[:, ln] = (k2 * w_inv).astype(BF16)
        v_o[:, ln] = v.astype(BF16)


def _prep(z, z_init, convw, mur, mul, w0, a0, k_k, k_a, r_k, wdec, aup, gup, ones_bd, tm):
    bsz, seq = z.shape[:2]
    g3 = 3 * GROUP_W
    lora_blk = 2 * g3 // LORA_PAD
    t_of = jnp.arange(tm)
    same_chunk = (t_of[:, None] // CHUNK) == (t_of[None, :] // CHUNK)
    tri = jnp.logical_and(same_chunk, t_of[None, :] <= t_of[:, None]).astype(BF16)

    halo_idx = lambda i: jnp.maximum(i * (tm // HALO) - 1, 0)
    row = lambda w, c: pl.BlockSpec((None, tm, w), lambda b, i: (b, i, c))
    halo = lambda w, c: pl.BlockSpec((None, HALO, w), lambda b, i: (b, halo_idx(i), c))
    init = lambda w, c: pl.BlockSpec((HALO, w), lambda b, i: (0, c))
    full = lambda a: pl.BlockSpec(a.shape, lambda b, i: (0,) * a.ndim)
    consts = (convw, mur, mul, w0, a0, k_k, k_a, r_k, wdec, aup, gup, ones_bd, tri)
    out_bf = jax.ShapeDtypeStruct((bsz, seq, GROUP_W), BF16)
    out_spec = pl.BlockSpec((None, tm, GROUP_W), lambda b, i: (b, i, 0))
    w_shape = jax.ShapeDtypeStruct((bsz, seq // CHUNK, 1, GROUP_W), F32)
    w_spec = pl.BlockSpec((None, tm // CHUNK, 1, GROUP_W), lambda b, i: (b, i, 0, 0))
    return pl.pallas_call(
        _prep_kernel,
        grid=(bsz, seq // tm),
        in_specs=[row(g3, 0), row(g3, 1), row(LORA_PAD, lora_blk),
                  halo(g3, 0), halo(g3, 1), halo(LORA_PAD, lora_blk),
                  init(g3, 0), init(g3, 1), init(LORA_PAD, lora_blk)] + [full(c) for c in consts],
        out_specs=[out_spec] * 6 + [w_spec] + [out_spec] * 2,
        out_shape=[out_bf] * 6 + [w_shape] + [out_bf] * 2,
        compiler_params=_params(("parallel", "arbitrary")),
        name="prep",
    )(z, z, z, z, z, z, z_init, z_init, z_init, *consts)


def _block_diag(x, mask):
    return jnp.where(mask, jnp.concatenate([x] * HEADS_PER_GROUP, axis=0), 0.0).astype(BF16)


def _scan_chain(chain_id, chunk, ready, rows, lanes, at_ref, rt_ref, bt_ref, kt_ref, v_ref, w_ref,
                bonus_ref, gate_ref, lnw_ref, lnb_ref, y_ref, state_ref):
    f = lambda ref: ref[rows, lanes].astype(F32)
    at, rt, bt, kt, v = map(f, (at_ref, rt_ref, bt_ref, kt_ref, v_ref))

    shape_bd = (SCAN_W, SCAN_W)
    bd_mask = (lax.broadcasted_iota(jnp.int32, shape_bd, 0) // HEAD_N
               == lax.broadcasted_iota(jnp.int32, shape_bd, 1) // HEAD_N)
    shape_c = (CHUNK, SCAN_W)
    t_idx = lax.broadcasted_iota(jnp.int32, shape_c, 0)
    s_idx = lax.broadcasted_iota(jnp.int32, shape_c, 1) % CHUNK
    bd = lambda x: _block_diag(x, bd_mask)
    stack = lambda a, b: jnp.concatenate([a, b], axis=0).astype(BF16)

    ar = stack(at, rt)
    ab = _dot_nt(ar, bd(bt))
    ak = _dot_nt(ar, bd(kt))
    yield
    a_ab = jnp.where(s_idx < t_idx, ab[:CHUNK], 0.0)
    a_rb = jnp.where(s_idx <= t_idx, ab[CHUNK:], 0.0)
    a_ak = jnp.where(s_idx < t_idx, ak[:CHUNK], 0.0)
    a_rk = jnp.where(s_idx <= t_idx, ak[CHUNK:], 0.0)

    inv = jnp.where(s_idx == t_idx, 1.0, 0.0) + a_ab
    pw = _dot(a_ab.astype(BF16), bd(a_ab))
    akv = _dot(stack(a_ak, a_rk), bd(v))
    yield
    for _ in range(int(math.log2(CHUNK)) - 2):
        both = _dot(stack(pw, inv), bd(pw))
        yield
        pw = both[:CHUNK]
        inv = inv + both[CHUNK:]
    inv = inv + _dot(inv.astype(BF16), bd(pw))
    assert chunk == 0 or (chain_id, chunk - 1) in ready
    state = state_ref[...]
    ars = _dot_nt(ar, state.astype(BF16))
    yield

    u = _dot(inv.astype(BF16), bd(ars[:CHUNK] + akv[:CHUNK]))
    yield
    y = ars[CHUNK:] + akv[CHUNK:] + _dot(a_rb.astype(BF16), bd(u))
    state_ref[...] = (state + jnp.where(bd_mask, _dot_tn(stack(u, v), stack(bt, kt)), 0.0)) * w_ref[chunk, :, lanes]
    ready.add((chain_id, chunk))
    yield

    ones_bd = jnp.where(bd_mask, 1.0, 0.0).astype(BF16)
    mean = _dot(y.astype(BF16), ones_bd) * (1.0 / HEAD_N)
    yield
    yc = y - mean
    var = _dot((yc * yc).astype(BF16), ones_bd) * (1.0 / HEAD_N)
    yield
    out = yc * lax.rsqrt(var + GN_EPS) * lnw_ref[:, lanes] + lnb_ref[:, lanes] + bonus_ref[rows, lanes]
    y_ref[rows, lanes] = (out * gate_ref[rows, lanes]).astype(BF16)


def _scan_kernel(*refs):
    seq_refs, (lnw_ref, lnb_ref, state0_ref, y_ref, state_out_ref, state_ref) = refs[:8], refs[8:]

    @pl.when(pl.program_id(0) == 0)
    def _():
        state_ref[...] = state0_ref[...]

    bsz, rows_per_step = y_ref.shape[:2]
    ready = set()

    def chunk_chains(c):
        rows = slice(c * CHUNK, (c + 1) * CHUNK)
        return [_scan_chain(b * N_GROUPS + g, c, ready, rows, slice(g * SCAN_W, (g + 1) * SCAN_W),
                            *[r.at[b] for r in seq_refs], lnw_ref, lnb_ref, y_ref.at[b],
                            state_ref.at[b * N_GROUPS + g])
                for b in range(bsz) for g in range(N_GROUPS)]

    pending = [chunk_chains(c) for c in range(rows_per_step // CHUNK)]
    chains, stage = [], 0
    while chains or pending:
        if pending and stage % SCAN_STAGGER == 0:
            chains = chains + pending.pop(0)
        chains = [c for c in chains if next(c, True) is None]
        stage += 1

    @pl.when(pl.program_id(0) == pl.num_programs(0) - 1)
    def _():
        state_out_ref[...] = state_ref[...]


def _scan(seq_arrays, lnw, lnb, state0):
    bsz, seq = seq_arrays[0].shape[:2]
    rows = SCAN_CHUNKS_PER_STEP * CHUNK if seq % (SCAN_CHUNKS_PER_STEP * CHUNK) == 0 else CHUNK
    blk = pl.BlockSpec((bsz, rows, GROUP_W), lambda c: (0, c, 0))
    wblk = pl.BlockSpec((bsz, rows // CHUNK, 1, GROUP_W), lambda c: (0, c, 0, 0))
    vec = pl.BlockSpec((1, GROUP_W), lambda c: (0, 0))
    st = pl.BlockSpec(state0.shape, lambda c: (0, 0, 0))
    return pl.pallas_call(
        _scan_kernel,
        grid=(seq // rows,),
        in_specs=[blk] * 5 + [wblk] + [blk] * 2 + [vec, vec, st],
        out_specs=[blk, st],
        out_shape=[jax.ShapeDtypeStruct((bsz, seq, GROUP_W), BF16), jax.ShapeDtypeStruct(state0.shape, F32)],
        scratch_shapes=[pltpu.VMEM(state0.shape, F32)],
        compiler_params=_params(("arbitrary",)),
        name="scan",
    )(*seq_arrays, lnw, lnb, state0)


def _outproj_kernel(yc_ref, yr_ref, h_ref, w_ref, gpost_ref, gpre_ref, *rest, cast):
    if cast:
        h1_ref, n2_ref, wb_ref = rest
        wb_ref[...] = w_ref[...].astype(BF16)
    else:
        wd_ref, h1_ref, n2_ref, wdb_ref = rest
        wb_ref = w_ref
        wdb_ref[...] = wd_ref[...].astype(BF16)

    sub = min(OUT_ROWS, yc_ref.shape[0])
    for r0 in range(0, yc_ref.shape[0], sub):
        rows = slice(r0, r0 + sub)
        mix = _dot(yc_ref[rows, :], wb_ref[:GROUP_W, :]) + _dot(yr_ref[rows, :], wb_ref[GROUP_W:, :])
        h1 = h_ref[rows, :] + _rms(mix, gpost_ref[...])
        h1_ref[rows, :] = h1
        n2_ref[rows, :] = _rms(h1, gpre_ref[...]).astype(BF16)


def _outproj(yc, yr, h2d, w, gpost, gpre, tm, w_down=None):
    cast = w_down is None
    m = h2d.shape[0]
    n = m // tm
    assert not cast or n == 1
    row_spec = pl.BlockSpec((tm, D_MODEL), lambda i: (i, 0))
    out_shape = [jax.ShapeDtypeStruct((m, D_MODEL), F32), jax.ShapeDtypeStruct((m, D_MODEL), BF16)]
    if cast:
        extra_in, extra_args = [], ()
        extra_out = pl.BlockSpec((D_MODEL, D_MODEL), lambda i: (0, 0))
        extra_shape = jax.ShapeDtypeStruct((D_MODEL, D_MODEL), BF16)
    else:
        slab = pl.BlockSpec((D_FF // n, D_MODEL), lambda i: (i, 0))
        extra_in, extra_args = [slab], (w_down,)
        extra_out, extra_shape = slab, jax.ShapeDtypeStruct((D_FF, D_MODEL), BF16)
    return pl.pallas_call(
        functools.partial(_outproj_kernel, cast=cast),
        grid=(m // tm,),
        in_specs=[
            pl.BlockSpec((tm, GROUP_W), lambda i: (i, 0)),
            pl.BlockSpec((tm, GROUP_W), lambda i: (i, 0)),
            pl.BlockSpec((tm, D_MODEL), lambda i: (i, 0)),
            pl.BlockSpec((D_MODEL, D_MODEL), lambda i: (0, 0), pipeline_mode=pl.Buffered(1)),
            pl.BlockSpec((1, D_MODEL), lambda i: (0, 0)),
            pl.BlockSpec((1, D_MODEL), lambda i: (0, 0)),
        ] + extra_in,
        out_specs=[row_spec, row_spec, extra_out],
        out_shape=out_shape + [extra_shape],
        compiler_params=_params(("parallel",)),
        name="outproj",
    )(yc, yr, h2d, w, gpost, gpre, *extra_args)


def _ffn_up_kernel(n2_ref, wg_ref, wu_ref, cw_ref, n2pre_ref, act_ref, wgb_ref, wub_ref, carry0_ref, carry_ref):
    b, t = pl.program_id(1), pl.program_id(2)

    @pl.when(jnp.logical_and(b == 0, t == 0))
    def _():
        wgb_ref[...] = wg_ref[...].astype(BF16)
        wub_ref[...] = wu_ref[...].astype(BF16)
        pre_gate = _dot(n2pre_ref[...], wgb_ref[...])
        carry0_ref[...] = pre_gate[pre_gate.shape[0] - HALO:, :]

    @pl.when(t == 0)
    def _():
        carry_ref[...] = carry0_ref[...]

    cw = cw_ref[...]
    for r0 in range(0, n2_ref.shape[0], UP_ROWS):
        rows = slice(r0, r0 + UP_ROWS)
        x = n2_ref[rows, :]
        gate = _dot(x, wgb_ref[...])
        up = _dot(x, wub_ref[...])
        xs = jnp.concatenate([carry_ref[...], gate], axis=0)
        carry_ref[...] = gate[UP_ROWS - HALO:, :]
        gc = pltpu.roll(xs, 2, 0)[HALO:] * cw[0:1] + pltpu.roll(xs, 1, 0)[HALO:] * cw[1:2] + gate * cw[2:3]
        act_ref[rows, :] = (gc * jax.nn.sigmoid(gc) * up).astype(BF16)


def _ffn_up(n2, w_gate_up, conv_ffn, n2_pre):
    bsz, seq = n2.shape[:2]
    n_tiles = D_FF // TN_UP
    return pl.pallas_call(
        _ffn_up_kernel,
        grid=(n_tiles, bsz, seq // TM_UP),
        in_specs=[
            pl.BlockSpec((None, TM_UP, D_MODEL), lambda j, b, t: (b, t, 0)),
            pl.BlockSpec((D_MODEL, TN_UP), lambda j, b, t: (0, j)),
            pl.BlockSpec((D_MODEL, TN_UP), lambda j, b, t: (0, j + n_tiles)),
            pl.BlockSpec((3, TN_UP), lambda j, b, t: (0, j)),
            pl.BlockSpec(n2_pre.shape, lambda j, b, t: (0, 0)),
        ],
        out_specs=pl.BlockSpec((None, TM_UP, TN_UP), lambda j, b, t: (b, t, j)),
        out_shape=jax.ShapeDtypeStruct((bsz, seq, D_FF), BF16),
        scratch_shapes=[pltpu.VMEM((D_MODEL, TN_UP), BF16), pltpu.VMEM((D_MODEL, TN_UP), BF16),
                        pltpu.VMEM((HALO, TN_UP), F32), pltpu.VMEM((HALO, TN_UP), F32)],
        compiler_params=_params(("arbitrary", "arbitrary", "arbitrary")),
        name="ffn_up",
    )(n2, w_gate_up, w_gate_up, conv_ffn, n2_pre)


def _ffn_down_kernel(act_ref, w_ref, h1_ref, g_ref, out_ref):
    k, last = pl.program_id(1), pl.num_programs(1) - 1

    @pl.when(k == 0)
    def _():
        out_ref[...] = _dot(act_ref[...], w_ref[...])

    @pl.when(jnp.logical_and(k > 0, k < last))
    def _():
        out_ref[...] += _dot(act_ref[...], w_ref[...])

    @pl.when(k == last)
    def _():
        for r0 in range(0, out_ref.shape[0], DOWN_ROWS):
            rows = slice(r0, r0 + DOWN_ROWS)
            f = out_ref[rows, :] + _dot(act_ref[rows, :], w_ref[...])
            out_ref[rows, :] = h1_ref[rows, :] + _rms(f, g_ref[...])


def _ffn_down(act, w, h1, g):
    m = act.shape[0]
    tm = TM_DOWN
    return pl.pallas_call(
        _ffn_down_kernel,
        grid=(m // tm, D_FF // TK_DOWN),
        in_specs=[
            pl.BlockSpec((tm, TK_DOWN), lambda i, k: (i, k)),
            pl.BlockSpec((TK_DOWN, D_MODEL), lambda i, k: (k, 0)),
            pl.BlockSpec((tm, D_MODEL), lambda i, k: (i, 0)),
            pl.BlockSpec((1, D_MODEL), lambda i, k: (0, 0)),
        ],
        out_specs=pl.BlockSpec((tm, D_MODEL), lambda i, k: (i, 0)),
        out_shape=jax.ShapeDtypeStruct((m, D_MODEL), F32),
        compiler_params=_params(("parallel", "arbitrary")),
        name="ffn_down",
    )(act, w, h1, g)


def kernel(x, meta_tokens, g_pre_mix, w_in, conv_w, mu_rwkv, w_decay_up, w0, a_up, a0, g_up, k_k, k_a,
           r_k, ln_x_w, ln_x_b, w_out, g_post_mix, g_pre_ffn, w_gate_up, conv_ffn, w_down, g_post_ffn):
    bsz, seq, _ = x.shape
    assert x.shape[2] == D_MODEL and meta_tokens.shape == (N_META, D_MODEL)
    row = lambda p: p.reshape(1, -1).astype(F32)
    g3 = 3 * GROUP_W

    def window_rows(w, start, win):
        return jnp.pad(w, ((start - win[0], win[1] - start - w.shape[0]), (0, 0))).astype(BF16)

    mu_r, mu_l = row(mu_rwkv[:g3]), row(jnp.pad(mu_rwkv[g3:], (0, LORA_PAD - LORA_W)))
    wdec = window_rows(w_decay_up, 0, WD_WIN)
    aup = window_rows(a_up, DECAY_LORA, AD_WIN)
    gup = window_rows(g_up, DECAY_LORA + AAA_LORA, GD_WIN)
    head_of = jnp.arange(PREP_SLAB) // HEAD_N
    ones_bd = (head_of[:, None] == head_of[None, :]).astype(BF16)
    prep_consts = (conv_w.astype(F32), mu_r, mu_l, row(w0), row(a0), row(k_k), row(k_a), row(r_k),
                   wdec, aup, gup, ones_bd)
    gpre, gpost, gffn = row(g_pre_mix), row(g_post_mix), row(g_pre_ffn)
    lnw, lnb = row(ln_x_w), row(ln_x_b)

    pre = jnp.concatenate([jnp.zeros((PREFIX - N_META, D_MODEL), F32), meta_tokens.astype(F32)], axis=0)
    z_pre, w_in_b = _inproj(pre, gpre, w_in.T, PREFIX, TN_PROJ_PRE, cast=True)
    pre_out = _prep(z_pre[None], jnp.zeros((HALO, IN_COLS_PAD), F32), *prep_consts, PREFIX)
    y_pre, state_pre = _scan(pre_out[1:], lnw, lnb, jnp.zeros((N_GROUPS, SCAN_W, SCAN_W), F32))
    _, n2_pre, w_out_b = _outproj(pre_out[0][0], y_pre[0], pre, w_out, gpost, gffn, PREFIX)

    m = bsz * seq
    x2d = x.reshape(m, D_MODEL)
    z = _inproj(x2d, gpre, w_in_b, TM_PROJ, TN_PROJ, cast=False).reshape(bsz, seq, IN_COLS_PAD)
    main_out = _prep(z, z_pre[PREFIX - HALO:], *prep_consts, TM_PREP)
    y_rwkv, _ = _scan(main_out[1:], lnw, lnb, jnp.tile(state_pre, (bsz, 1, 1)))
    h1, n2, w_down_b = _outproj(main_out[0].reshape(m, GROUP_W), y_rwkv.reshape(m, GROUP_W), x2d, w_out_b, gpost,
                                gffn, TM_OUT, w_down)
    act = _ffn_up(n2.reshape(bsz, seq, D_MODEL), w_gate_up, conv_ffn.astype(F32), n2_pre[PREFIX - N_META:])
    out = _ffn_down(act.reshape(m, D_FF), w_down_b, h1, row(g_post_ffn))
    return out.reshape(bsz, seq, D_MODEL)
```

```python
import functools
import math

import jax
import jax.numpy as jnp
from jax import lax
from jax.experimental import pallas as pl
from jax.experimental.pallas import tpu as pltpu

F32 = jnp.float32
BF16 = jnp.bfloat16

D_MODEL = 2048
N_META = 16
GROUP_W = 1024
HEAD_N = 64
DECAY_LORA = 96
AAA_LORA = 96
GATE_LORA = 256
LANE = 128
LORA_W = DECAY_LORA + AAA_LORA + GATE_LORA
LORA_PAD = 512
WD_WIN = (0, LANE)
AD_WIN = (0, 2 * LANE)
GD_WIN = (LANE, LORA_PAD)
IN_COLS = 3 * GROUP_W + 3 * GROUP_W + LORA_W
IN_COLS_PAD = IN_COLS - LORA_W + LORA_PAD
D_FF = 5632
RMS_EPS = 1e-6
GN_EPS = 64e-5
DECAY_SCALE = math.exp(-0.5)

CHUNK = 64
assert CHUNK == HEAD_N
PREFIX = CHUNK
HALO = 8
HEADS_PER_GROUP = 4
SCAN_W = HEADS_PER_GROUP * HEAD_N
N_GROUPS = GROUP_W // SCAN_W
SCAN_CHUNKS_PER_STEP = 4
SCAN_STAGGER = 2

VMEM_LIMIT = 56 * 1024 * 1024

TM_PROJ = 1024
TN_PROJ = IN_COLS_PAD // 4
TN_PROJ_PRE = 512
TM_PREP = 512
PREP_SLAB = 256
TM_OUT = 512
OUT_ROWS = 128
TM_UP = 2048
UP_ROWS = 1024
TN_UP = 512
TM_DOWN = 1024
TK_DOWN = D_FF // 4
DOWN_ROWS = 256
assert D_FF // TK_DOWN > 1
PROJ_ROWS = 256


def _dot(a, b):
    return jnp.dot(a, b, preferred_element_type=F32)


def _dot_nt(a, b):
    return lax.dot_general(a, b, (((1,), (1,)), ((), ())), preferred_element_type=F32)


def _dot_tn(a, b):
    return lax.dot_general(a, b, (((0,), (0,)), ((), ())), preferred_element_type=F32)


def _rms(x, g):
    return x * lax.rsqrt(jnp.mean(x * x, axis=-1, keepdims=True) + RMS_EPS) * g


def _params(sem):
    return pltpu.CompilerParams(dimension_semantics=sem, vmem_limit_bytes=VMEM_LIMIT)


def _inproj_kernel(h_ref, g_ref, w_ref, z_ref, *rest, cast):
    xn_ref = rest[-1]
    j = pl.program_id(1)

    if cast:
        @pl.when(j == 0)
        def _():
            xn_ref[...] = _rms(h_ref[...], g_ref[...]).astype(BF16)

        tn = w_ref.shape[0]
        col = j * tn + lax.broadcasted_iota(jnp.int32, (tn, 1), 0)
        rest[0][...] = jnp.where(col < IN_COLS, w_ref[...], 0.0).T.astype(BF16)
        z_ref[...] = _dot(xn_ref[...], rest[0][...])
        return

    @pl.when(j == 0)
    def _():
        for r0 in range(0, h_ref.shape[0], PROJ_ROWS):
            rows = slice(r0, r0 + PROJ_ROWS)
            xn_ref[rows, :] = _rms(h_ref[rows, :], g_ref[...]).astype(BF16)
            z_ref[rows, :] = _dot(xn_ref[rows, :], w_ref[...])

    @pl.when(j > 0)
    def _():
        z_ref[...] = _dot(xn_ref[...], w_ref[...])


def _inproj(h2d, g, w, tm, tn, cast):
    m = h2d.shape[0]
    z_spec = pl.BlockSpec((tm, tn), lambda i, j: (i, j))
    z_shape = jax.ShapeDtypeStruct((m, IN_COLS_PAD), F32)
    w_spec = pl.BlockSpec((D_MODEL, tn), lambda i, j: (0, j))
    return pl.pallas_call(
        functools.partial(_inproj_kernel, cast=cast),
        grid=(m // tm, IN_COLS_PAD // tn),
        in_specs=[
            pl.BlockSpec((tm, D_MODEL), lambda i, j: (i, 0)),
            pl.BlockSpec((1, D_MODEL), lambda i, j: (0, 0)),
            pl.BlockSpec((tn, D_MODEL), lambda i, j: (j, 0)) if cast else w_spec,
        ],
        out_specs=[z_spec, w_spec] if cast else z_spec,
        out_shape=[z_shape, jax.ShapeDtypeStruct((D_MODEL, IN_COLS_PAD), BF16)] if cast else z_shape,
        scratch_shapes=[pltpu.VMEM((tm, D_MODEL), BF16)],
        compiler_params=_params(("parallel", "arbitrary")),
        name="inproj",
    )(h2d, g, w)


def _prev_rows(cur, halo, shift):
    xs = jnp.concatenate([halo, cur], axis=0)
    return pltpu.roll(xs, shift, 0)[HALO:]


def _split_dot(m01, x):
    x1 = x.astype(BF16)
    r1 = x - x1.astype(F32)
    x2 = r1.astype(BF16)
    x3 = (r1 - x2.astype(F32)).astype(BF16)
    return _dot(m01, x1) + _dot(m01, x2) + _dot(m01, x3)


def _prep_kernel(zc_ref, zr_ref, zl_ref, hc_ref, hr_ref, hl_ref, ic_ref, ir_ref, il_ref,
                 convw_ref, mur_ref, mul_ref, w0_ref, a0_ref, kk_ref, ka_ref, rk_ref,
                 wdec_ref, aup_ref, gup_ref, ones_ref, tri_ref,
                 yconv_o, at_o, rt_o, bt_o, kt_o, v_o, w_o, bonus_o, gate_o):
    first = pl.program_id(1) == 0
    g, w = GROUP_W, PREP_SLAB
    halo_of = lambda init_ref, halo_ref, c: jnp.where(first, init_ref[:, c], halo_ref[:, c])

    zl = zl_ref[...]
    zsl = zl + (_prev_rows(zl, halo_of(il_ref, hl_ref, slice(None)), 1) - zl) * mul_ref[...]
    wd = jnp.tanh(zsl[:, WD_WIN[0]:WD_WIN[1]]).astype(BF16)
    ad = zsl[:, AD_WIN[0]:AD_WIN[1]].astype(BF16)
    gd = jax.nn.sigmoid(zsl[:, GD_WIN[0]:GD_WIN[1]]).astype(BF16)
    ones_bd = ones_ref[...]
    tri = tri_ref[...]
    cw = convw_ref[...]

    for s in range(g // w):
        ln = slice(s * w, (s + 1) * w)
        col = lambda sec: slice(sec * g + s * w, sec * g + (s + 1) * w)

        p = zc_ref[:, col(1)] * zc_ref[:, col(2)]
        ph = halo_of(ic_ref, hc_ref, col(1)) * halo_of(ic_ref, hc_ref, col(2))
        conv = _prev_rows(p, ph, 2) * cw[0:1, ln] + _prev_rows(p, ph, 1) * cw[1:2, ln] + p * cw[2:3, ln]
        yconv_o[:, ln] = (zc_ref[:, col(0)] * conv).astype(BF16)

        def shifted(sec):
            z = zr_ref[:, col(sec)]
            return z + (_prev_rows(z, halo_of(ir_ref, hr_ref, col(sec)), 1) - z) * mur_ref[:, col(sec)]

        r, k, v = shifted(0), shifted(1), shifted(2)
        log_w = -DECAY_SCALE * jax.nn.sigmoid(w0_ref[:, ln] + _dot(wd, wdec_ref[:, ln]))
        a = jax.nn.sigmoid(a0_ref[:, ln] + _dot(ad, aup_ref[:, ln]))
        gate_o[:, ln] = _dot(gd, gup_ref[:, ln]).astype(BF16)

        kk = k * kk_ref[:, ln]
        kk = kk * lax.rsqrt(jnp.maximum(_dot((kk * kk).astype(BF16), ones_bd), 1e-24))
        k2 = k * (1.0 + (a - 1.0) * ka_ref[:, ln])
        bonus_o[:, ln] = (_dot((r * k2 * rk_ref[:, ln]).astype(BF16), ones_bd) * v).astype(BF16)
        bv = kk * a

        cum = _split_dot(tri, log_w)
        w_inc = jnp.exp(cum)
        w_inv = jnp.exp(-cum)
        for c in range(w_o.shape[0]):
            w_o[c, :, ln] = w_inc[(c + 1) * CHUNK - 1:(c + 1) * CHUNK, :]
        at_o[:, ln] = (-kk * jnp.exp(cum - log_w)).astype(BF16)
        rt_o[:, ln] = (r * w_inc).astype(BF16)
        bt_o[:, ln] = (bv * w_inv).astype(BF16)
        kt_o[:, ln] = (k2 * w_inv).astype(BF16)
        v_o[:, ln] = v.astype(BF16)


def _prep(z, z_init, convw, mur, mul, w0, a0, k_k, k_a, r_k, wdec, aup, gup, ones_bd, tm):
    bsz, seq = z.shape[:2]
    g3 = 3 * GROUP_W
    lora_blk = 2 * g3 // LORA_PAD
    t_of = jnp.arange(tm)
    same_chunk = (t_of[:, None] // CHUNK) == (t_of[None, :] // CHUNK)
    tri = jnp.logical_and(same_chunk, t_of[None, :] <= t_of[:, None]).astype(BF16)

    halo_idx = lambda i: jnp.maximum(i * (tm // HALO) - 1, 0)
    row = lambda w, c: pl.BlockSpec((None, tm, w), lambda b, i: (b, i, c))
    halo = lambda w, c: pl.BlockSpec((None, HALO, w), lambda b, i: (b, halo_idx(i), c))
    init = lambda w, c: pl.BlockSpec((HALO, w), lambda b, i: (0, c))
    full = lambda a: pl.BlockSpec(a.shape, lambda b, i: (0,) * a.ndim)
    consts = (convw, mur, mul, w0, a0, k_k, k_a, r_k, wdec, aup, gup, ones_bd, tri)
    out_bf = jax.ShapeDtypeStruct((bsz, seq, GROUP_W), BF16)
    out_spec = pl.BlockSpec((None, tm, GROUP_W), lambda b, i: (b, i, 0))
    w_shape = jax.ShapeDtypeStruct((bsz, seq // CHUNK, 1, GROUP_W), F32)
    w_spec = pl.BlockSpec((None, tm // CHUNK, 1, GROUP_W), lambda b, i: (b, i, 0, 0))
    return pl.pallas_call(
        _prep_kernel,
        grid=(bsz, seq // tm),
        in_specs=[row(g3, 0), row(g3, 1), row(LORA_PAD, lora_blk),
                  halo(g3, 0), halo(g3, 1), halo(LORA_PAD, lora_blk),
                  init(g3, 0), init(g3, 1), init(LORA_PAD, lora_blk)] + [full(c) for c in consts],
        out_specs=[out_spec] * 6 + [w_spec] + [out_spec] * 2,
        out_shape=[out_bf] * 6 + [w_shape] + [out_bf] * 2,
        compiler_params=_params(("parallel", "arbitrary")),
        name="prep",
    )(z, z, z, z, z, z, z_init, z_init, z_init, *consts)


def _block_diag(x, mask):
    return jnp.where(mask, jnp.concatenate([x] * HEADS_PER_GROUP, axis=0), 0.0).astype(BF16)


def _scan_chain(chain_id, chunk, ready, rows, lanes, at_ref, rt_ref, bt_ref, kt_ref, v_ref, w_ref,
                bonus_ref, gate_ref, lnw_ref, lnb_ref, y_ref, state_ref):
    at, rt, bt, kt, v = (ref[rows, lanes] for ref in (at_ref, rt_ref, bt_ref, kt_ref, v_ref))

    shape_bd = (SCAN_W, SCAN_W)
    bd_mask = (lax.broadcasted_iota(jnp.int32, shape_bd, 0) // HEAD_N
               == lax.broadcasted_iota(jnp.int32, shape_bd, 1) // HEAD_N)
    shape_c = (CHUNK, SCAN_W)
    t_idx = lax.broadcasted_iota(jnp.int32, shape_c, 0)
    s_idx = lax.broadcasted_iota(jnp.int32, shape_c, 1) % CHUNK
    bd = lambda x: _block_diag(x.astype(F32), bd_mask)
    stack = lambda a, b: jnp.concatenate([a.astype(BF16), b.astype(BF16)], axis=0)

    ar = stack(at, rt)
    ab = _dot_nt(ar, bd(bt))
    ak = _dot_nt(ar, bd(kt))
    yield
    a_ab = jnp.where(s_idx < t_idx, ab[:CHUNK], 0.0)
    a_rb = jnp.where(s_idx <= t_idx, ab[CHUNK:], 0.0).astype(BF16)
    a_k = stack(jnp.where(s_idx < t_idx, ak[:CHUNK], 0.0), jnp.where(s_idx <= t_idx, ak[CHUNK:], 0.0))

    inv = jnp.where(s_idx == t_idx, 1.0, 0.0) + a_ab
    pw = _dot(a_ab.astype(BF16), bd(a_ab))
    yield
    for _ in range(int(math.log2(CHUNK)) - 2):
        both = _dot(stack(pw, inv), bd(pw))
        yield
        pw = both[:CHUNK]
        inv = inv + both[CHUNK:]
    inv = inv + _dot(inv.astype(BF16), bd(pw))
    assert chunk == 0 or (chain_id, chunk - 1) in ready
    state = state_ref[...]
    ars = _dot_nt(ar, state.astype(BF16))
    akv = _dot(a_k, bd(v))
    yield

    u = _dot(inv.astype(BF16), bd(ars[:CHUNK] + akv[:CHUNK]))
    yield
    y = ars[CHUNK:] + akv[CHUNK:] + _dot(a_rb, bd(u))
    state_ref[...] = (state + jnp.where(bd_mask, _dot_tn(stack(u, v), stack(bt, kt)), 0.0)) * w_ref[chunk, :, lanes]
    ready.add((chain_id, chunk))
    yield

    ones_bd = jnp.where(bd_mask, 1.0, 0.0).astype(BF16)
    mean = _dot(y.astype(BF16), ones_bd) * (1.0 / HEAD_N)
    yield
    yc = y - mean
    var = _dot((yc * yc).astype(BF16), ones_bd) * (1.0 / HEAD_N)
    yield
    out = yc * lax.rsqrt(var + GN_EPS) * lnw_ref[:, lanes] + lnb_ref[:, lanes] + bonus_ref[rows, lanes]
    y_ref[rows, lanes] = (out * gate_ref[rows, lanes]).astype(BF16)


def _scan_kernel(*refs):
    seq_refs, (lnw_ref, lnb_ref, state0_ref, y_ref, state_out_ref, state_ref) = refs[:8], refs[8:]

    @pl.when(pl.program_id(0) == 0)
    def _():
        state_ref[...] = state0_ref[...]

    bsz, rows_per_step = y_ref.shape[:2]
    ready = set()

    def chunk_chains(c):
        rows = slice(c * CHUNK, (c + 1) * CHUNK)
        return [_scan_chain(b * N_GROUPS + g, c, ready, rows, slice(g * SCAN_W, (g + 1) * SCAN_W),
                            *[r.at[b] for r in seq_refs], lnw_ref, lnb_ref, y_ref.at[b],
                            state_ref.at[b * N_GROUPS + g])
                for b in range(bsz) for g in range(N_GROUPS)]

    pending = [chunk_chains(c) for c in range(rows_per_step // CHUNK)]
    chains, stage = [], 0
    while chains or pending:
        if pending and stage % SCAN_STAGGER == 0:
            chains = chains + pending.pop(0)
        chains = [c for c in chains if next(c, True) is None]
        stage += 1

    @pl.when(pl.program_id(0) == pl.num_programs(0) - 1)
    def _():
        state_out_ref[...] = state_ref[...]


def _scan(seq_arrays, lnw, lnb, state0):
    bsz, seq = seq_arrays[0].shape[:2]
    rows = SCAN_CHUNKS_PER_STEP * CHUNK if seq % (SCAN_CHUNKS_PER_STEP * CHUNK) == 0 else CHUNK
    blk = pl.BlockSpec((bsz, rows, GROUP_W), lambda c: (0, c, 0))
    wblk = pl.BlockSpec((bsz, rows // CHUNK, 1, GROUP_W), lambda c: (0, c, 0, 0))
    vec = pl.BlockSpec((1, GROUP_W), lambda c: (0, 0))
    st = pl.BlockSpec(state0.shape, lambda c: (0, 0, 0))
    return pl.pallas_call(
        _scan_kernel,
        grid=(seq // rows,),
        in_specs=[blk] * 5 + [wblk] + [blk] * 2 + [vec, vec, st],
        out_specs=[blk, st],
        out_shape=[jax.ShapeDtypeStruct((bsz, seq, GROUP_W), BF16), jax.ShapeDtypeStruct(state0.shape, F32)],
        scratch_shapes=[pltpu.VMEM(state0.shape, F32)],
        compiler_params=_params(("arbitrary",)),
        name="scan",
    )(*seq_arrays, lnw, lnb, state0)


def _outproj_kernel(yc_ref, yr_ref, h_ref, w_ref, gpost_ref, gpre_ref, *rest, cast):
    if cast:
        h1_ref, n2_ref, wb_ref = rest
        wb_ref[...] = w_ref[...].astype(BF16)
    else:
        wd_ref, h1_ref, n2_ref, wdb_ref = rest
        wb_ref = w_ref
        wdb_ref[...] = wd_ref[...].astype(BF16)

    sub = min(OUT_ROWS, yc_ref.shape[0])
    for r0 in range(0, yc_ref.shape[0], sub):
        rows = slice(r0, r0 + sub)
        mix = _dot(yc_ref[rows, :], wb_ref[:GROUP_W, :]) + _dot(yr_ref[rows, :], wb_ref[GROUP_W:, :])
        h1 = h_ref[rows, :] + _rms(mix, gpost_ref[...])
        h1_ref[rows, :] = h1
        n2_ref[rows, :] = _rms(h1, gpre_ref[...]).astype(BF16)


def _outproj(yc, yr, h2d, w, gpost, gpre, tm, w_down=None):
    cast = w_down is None
    m = h2d.shape[0]
    n = m // tm
    assert not cast or n == 1
    row_spec = pl.BlockSpec((tm, D_MODEL), lambda i: (i, 0))
    out_shape = [jax.ShapeDtypeStruct((m, D_MODEL), F32), jax.ShapeDtypeStruct((m, D_MODEL), BF16)]
    if cast:
        extra_in, extra_args = [], ()
        extra_out = pl.BlockSpec((D_MODEL, D_MODEL), lambda i: (0, 0))
        extra_shape = jax.ShapeDtypeStruct((D_MODEL, D_MODEL), BF16)
    else:
        slab = pl.BlockSpec((D_FF // n, D_MODEL), lambda i: (i, 0))
        extra_in, extra_args = [slab], (w_down,)
        extra_out, extra_shape = slab, jax.ShapeDtypeStruct((D_FF, D_MODEL), BF16)
    return pl.pallas_call(
        functools.partial(_outproj_kernel, cast=cast),
        grid=(m // tm,),
        in_specs=[
            pl.BlockSpec((tm, GROUP_W), lambda i: (i, 0)),
            pl.BlockSpec((tm, GROUP_W), lambda i: (i, 0)),
            pl.BlockSpec((tm, D_MODEL), lambda i: (i, 0)),
            pl.BlockSpec((D_MODEL, D_MODEL), lambda i: (0, 0), pipeline_mode=pl.Buffered(1)),
            pl.BlockSpec((1, D_MODEL), lambda i: (0, 0)),
            pl.BlockSpec((1, D_MODEL), lambda i: (0, 0)),
        ] + extra_in,
        out_specs=[row_spec, row_spec, extra_out],
        out_shape=out_shape + [extra_shape],
        compiler_params=_params(("parallel",)),
        name="outproj",
    )(yc, yr, h2d, w, gpost, gpre, *extra_args)


def _ffn_up_kernel(n2_ref, wg_ref, wu_ref, cw_ref, n2pre_ref, act_ref, wgb_ref, wub_ref, carry0_ref, carry_ref):
    b, t = pl.program_id(1), pl.program_id(2)

    @pl.when(jnp.logical_and(b == 0, t == 0))
    def _():
        wgb_ref[...] = wg_ref[...].astype(BF16)
        wub_ref[...] = wu_ref[...].astype(BF16)
        pre_gate = _dot(n2pre_ref[...], wgb_ref[...])
        carry0_ref[...] = pre_gate[pre_gate.shape[0] - HALO:, :]

    @pl.when(t == 0)
    def _():
        carry_ref[...] = carry0_ref[...]

    cw = cw_ref[...]
    for r0 in range(0, n2_ref.shape[0], UP_ROWS):
        rows = slice(r0, r0 + UP_ROWS)
        x = n2_ref[rows, :]
        gate = _dot(x, wgb_ref[...])
        up = _dot(x, wub_ref[...])
        xs = jnp.concatenate([carry_ref[...], gate], axis=0)
        carry_ref[...] = gate[UP_ROWS - HALO:, :]
        gc = pltpu.roll(xs, 2, 0)[HALO:] * cw[0:1] + pltpu.roll(xs, 1, 0)[HALO:] * cw[1:2] + gate * cw[2:3]
        act_ref[rows, :] = (gc * jax.nn.sigmoid(gc) * up).astype(BF16)


def _ffn_up(n2, w_gate_up, conv_ffn, n2_pre):
    bsz, seq = n2.shape[:2]
    n_tiles = D_FF // TN_UP
    return pl.pallas_call(
        _ffn_up_kernel,
        grid=(n_tiles, bsz, seq // TM_UP),
        in_specs=[
            pl.BlockSpec((None, TM_UP, D_MODEL), lambda j, b, t: (b, t, 0)),
            pl.BlockSpec((D_MODEL, TN_UP), lambda j, b, t: (0, j)),
            pl.BlockSpec((D_MODEL, TN_UP), lambda j, b, t: (0, j + n_tiles)),
            pl.BlockSpec((3, TN_UP), lambda j, b, t: (0, j)),
            pl.BlockSpec(n2_pre.shape, lambda j, b, t: (0, 0)),
        ],
        out_specs=pl.BlockSpec((None, TM_UP, TN_UP), lambda j, b, t: (b, t, j)),
        out_shape=jax.ShapeDtypeStruct((bsz, seq, D_FF), BF16),
        scratch_shapes=[pltpu.VMEM((D_MODEL, TN_UP), BF16), pltpu.VMEM((D_MODEL, TN_UP), BF16),
                        pltpu.VMEM((HALO, TN_UP), F32), pltpu.VMEM((HALO, TN_UP), F32)],
        compiler_params=_params(("arbitrary", "arbitrary", "arbitrary")),
        name="ffn_up",
    )(n2, w_gate_up, w_gate_up, conv_ffn, n2_pre)


def _ffn_down_kernel(act_ref, w_ref, h1_ref, g_ref, out_ref):
    k, last = pl.program_id(1), pl.num_programs(1) - 1

    @pl.when(k == 0)
    def _():
        out_ref[...] = _dot(act_ref[...], w_ref[...])

    @pl.when(jnp.logical_and(k > 0, k < last))
    def _():
        out_ref[...] += _dot(act_ref[...], w_ref[...])

    @pl.when(k == last)
    def _():
        for r0 in range(0, out_ref.shape[0], DOWN_ROWS):
            rows = slice(r0, r0 + DOWN_ROWS)
            f = out_ref[rows, :] + _dot(act_ref[rows, :], w_ref[...])
            out_ref[rows, :] = h1_ref[rows, :] + _rms(f, g_ref[...])


def _ffn_down(act, w, h1, g):
    m = act.shape[0]
    tm = TM_DOWN
    return pl.pallas_call(
        _ffn_down_kernel,
        grid=(m // tm, D_FF // TK_DOWN),
        in_specs=[
            pl.BlockSpec((tm, TK_DOWN), lambda i, k: (i, k)),
            pl.BlockSpec((TK_DOWN, D_MODEL), lambda i, k: (k, 0)),
            pl.BlockSpec((tm, D_MODEL), lambda i, k: (i, 0)),
            pl.BlockSpec((1, D_MODEL), lambda i, k: (0, 0)),
        ],
        out_specs=pl.BlockSpec((tm, D_MODEL), lambda i, k: (i, 0)),
        out_shape=jax.ShapeDtypeStruct((m, D_MODEL), F32),
        compiler_params=_params(("parallel", "arbitrary")),
        name="ffn_down",
    )(act, w, h1, g)


def kernel(x, meta_tokens, g_pre_mix, w_in, conv_w, mu_rwkv, w_decay_up, w0, a_up, a0, g_up, k_k, k_a,
           r_k, ln_x_w, ln_x_b, w_out, g_post_mix, g_pre_ffn, w_gate_up, conv_ffn, w_down, g_post_ffn):
    bsz, seq, _ = x.shape
    assert x.shape[2] == D_MODEL and meta_tokens.shape == (N_META, D_MODEL)
    row = lambda p: p.reshape(1, -1).astype(F32)
    g3 = 3 * GROUP_W

    def window_rows(w, start, win):
        return jnp.pad(w, ((start - win[0], win[1] - start - w.shape[0]), (0, 0))).astype(BF16)

    mu_r, mu_l = row(mu_rwkv[:g3]), row(jnp.pad(mu_rwkv[g3:], (0, LORA_PAD - LORA_W)))
    wdec = window_rows(w_decay_up, 0, WD_WIN)
    aup = window_rows(a_up, DECAY_LORA, AD_WIN)
    gup = window_rows(g_up, DECAY_LORA + AAA_LORA, GD_WIN)
    head_of = jnp.arange(PREP_SLAB) // HEAD_N
    ones_bd = (head_of[:, None] == head_of[None, :]).astype(BF16)
    prep_consts = (conv_w.astype(F32), mu_r, mu_l, row(w0), row(a0), row(k_k), row(k_a), row(r_k),
                   wdec, aup, gup, ones_bd)
    gpre, gpost, gffn = row(g_pre_mix), row(g_post_mix), row(g_pre_ffn)
    lnw, lnb = row(ln_x_w), row(ln_x_b)

    pre = jnp.concatenate([jnp.zeros((PREFIX - N_META, D_MODEL), F32), meta_tokens.astype(F32)], axis=0)
    z_pre, w_in_b = _inproj(pre, gpre, w_in.T, PREFIX, TN_PROJ_PRE, cast=True)
    pre_out = _prep(z_pre[None], jnp.zeros((HALO, IN_COLS_PAD), F32), *prep_consts, PREFIX)
    y_pre, state_pre = _scan(pre_out[1:], lnw, lnb, jnp.zeros((N_GROUPS, SCAN_W, SCAN_W), F32))
    _, n2_pre, w_out_b = _outproj(pre_out[0][0], y_pre[0], pre, w_out, gpost, gffn, PREFIX)

    m = bsz * seq
    x2d = x.reshape(m, D_MODEL)
    z = _inproj(x2d, gpre, w_in_b, TM_PROJ, TN_PROJ, cast=False).reshape(bsz, seq, IN_COLS_PAD)
    main_out = _prep(z, z_pre[PREFIX - HALO:], *prep_consts, TM_PREP)
    y_rwkv, _ = _scan(main_out[1:], lnw, lnb, jnp.tile(state_pre, (bsz, 1, 1)))
    h1, n2, w_down_b = _outproj(main_out[0].reshape(m, GROUP_W), y_rwkv.reshape(m, GROUP_W), x2d, w_out_b, gpost,
                                gffn, TM_OUT, w_down)
    act = _ffn_up(n2.reshape(bsz, seq, D_MODEL), w_gate_up, conv_ffn.astype(F32), n2_pre[PREFIX - N_META:])
    out = _ffn_down(act.reshape(m, D_FF), w_down_b, h1, row(g_post_ffn))
    return out.reshape(bsz, seq, D_MODEL)
```

```python
import functools
import math

import jax
import jax.numpy as jnp
from jax import lax
from jax.experimental import pallas as pl
from jax.experimental.pallas import tpu as pltpu

F32 = jnp.float32
BF16 = jnp.bfloat16

D_MODEL = 2048
N_META = 16
GROUP_W = 1024
HEAD_N = 64
DECAY_LORA = 96
AAA_LORA = 96
GATE_LORA = 256
LANE = 128
LORA_W = DECAY_LORA + AAA_LORA + GATE_LORA
LORA_PAD = 512
WD_WIN = (0, LANE)
AD_WIN = (0, 2 * LANE)
GD_WIN = (LANE, LORA_PAD)
IN_COLS = 3 * GROUP_W + 3 * GROUP_W + LORA_W
IN_COLS_PAD = IN_COLS - LORA_W + LORA_PAD
D_FF = 5632
RMS_EPS = 1e-6
GN_EPS = 64e-5
DECAY_SCALE = math.exp(-0.5)

CHUNK = 64
assert CHUNK == HEAD_N
PREFIX = CHUNK
HALO = 8
HEADS_PER_GROUP = 4
SCAN_W = HEADS_PER_GROUP * HEAD_N
N_GROUPS = GROUP_W // SCAN_W
SCAN_CHUNKS_PER_STEP = 4
SCAN_STAGGER = 2

VMEM_LIMIT = 56 * 1024 * 1024

TM_PROJ = 1024
TN_PROJ = IN_COLS_PAD // 4
TN_PROJ_PRE = 512
TM_PREP = 512
PREP_SLAB = 256
TM_OUT = 512
OUT_ROWS = 128
TM_UP = 2048
UP_ROWS = 1024
TN_UP = 512
TM_DOWN = 1024
TK_DOWN = D_FF // 4
DOWN_ROWS = 256
assert D_FF // TK_DOWN > 1
PROJ_ROWS = 256


def _dot(a, b):
    return jnp.dot(a, b, preferred_element_type=F32)


def _dot_nt(a, b):
    return lax.dot_general(a, b, (((1,), (1,)), ((), ())), preferred_element_type=F32)


def _dot_tn(a, b):
    return lax.dot_general(a, b, (((0,), (0,)), ((), ())), preferred_element_type=F32)


def _rms(x, g):
    return x * lax.rsqrt(jnp.mean(x * x, axis=-1, keepdims=True) + RMS_EPS) * g


def _params(sem):
    return pltpu.CompilerParams(dimension_semantics=sem, vmem_limit_bytes=VMEM_LIMIT)


def _inproj_kernel(h_ref, g_ref, w_ref, z_ref, *rest, cast):
    xn_ref = rest[-1]
    j = pl.program_id(1)

    if cast:
        @pl.when(j == 0)
        def _():
            xn_ref[...] = _rms(h_ref[...], g_ref[...]).astype(BF16)

        tn = w_ref.shape[0]
        col = j * tn + lax.broadcasted_iota(jnp.int32, (tn, 1), 0)
        rest[0][...] = jnp.where(col < IN_COLS, w_ref[...], 0.0).T.astype(BF16)
        z_ref[...] = _dot(xn_ref[...], rest[0][...])
        return

    @pl.when(j == 0)
    def _():
        for r0 in range(0, h_ref.shape[0], PROJ_ROWS):
            rows = slice(r0, r0 + PROJ_ROWS)
            xn_ref[rows, :] = _rms(h_ref[rows, :], g_ref[...]).astype(BF16)
            z_ref[rows, :] = _dot(xn_ref[rows, :], w_ref[...])

    @pl.when(j > 0)
    def _():
        z_ref[...] = _dot(xn_ref[...], w_ref[...])


def _inproj(h2d, g, w, tm, tn, cast):
    m = h2d.shape[0]
    z_spec = pl.BlockSpec((tm, tn), lambda i, j: (i, j))
    z_shape = jax.ShapeDtypeStruct((m, IN_COLS_PAD), F32)
    w_spec = pl.BlockSpec((D_MODEL, tn), lambda i, j: (0, j))
    return pl.pallas_call(
        functools.partial(_inproj_kernel, cast=cast),
        grid=(m // tm, IN_COLS_PAD // tn),
        in_specs=[
            pl.BlockSpec((tm, D_MODEL), lambda i, j: (i, 0)),
            pl.BlockSpec((1, D_MODEL), lambda i, j: (0, 0)),
            pl.BlockSpec((tn, D_MODEL), lambda i, j: (j, 0)) if cast else w_spec,
        ],
        out_specs=[z_spec, w_spec] if cast else z_spec,
        out_shape=[z_shape, jax.ShapeDtypeStruct((D_MODEL, IN_COLS_PAD), BF16)] if cast else z_shape,
        scratch_shapes=[pltpu.VMEM((tm, D_MODEL), BF16)],
        compiler_params=_params(("parallel", "arbitrary")),
        name="inproj",
    )(h2d, g, w)


def _prev_rows(cur, halo, shift):
    xs = jnp.concatenate([halo, cur], axis=0)
    return pltpu.roll(xs, shift, 0)[HALO:]


def _split_dot(m01, x):
    x1 = x.astype(BF16)
    x2 = (x - x1.astype(F32)).astype(BF16)
    return _dot(m01, x1) + _dot(m01, x2)


def _prep_kernel(zc_ref, zr_ref, zl_ref, hc_ref, hr_ref, hl_ref, ic_ref, ir_ref, il_ref,
                 convw_ref, mur_ref, mul_ref, w0_ref, a0_ref, kk_ref, ka_ref, rk_ref,
                 wdec_ref, aup_ref, gup_ref, ones_ref, tri_ref,
                 yconv_o, at_o, rt_o, bt_o, kt_o, v_o, w_o, bonus_o, gate_o):
    first = pl.program_id(1) == 0
    g, w = GROUP_W, PREP_SLAB
    halo_of = lambda init_ref, halo_ref, c: jnp.where(first, init_ref[:, c], halo_ref[:, c])

    zl = zl_ref[...]
    zsl = zl + (_prev_rows(zl, halo_of(il_ref, hl_ref, slice(None)), 1) - zl) * mul_ref[...]
    wd = jnp.tanh(zsl[:, WD_WIN[0]:WD_WIN[1]]).astype(BF16)
    ad = zsl[:, AD_WIN[0]:AD_WIN[1]].astype(BF16)
    gd = jax.nn.sigmoid(zsl[:, GD_WIN[0]:GD_WIN[1]]).astype(BF16)
    ones_bd = ones_ref[...]
    tri = tri_ref[...]
    cw = convw_ref[...]

    for s in range(g // w):
        ln = slice(s * w, (s + 1) * w)
        col = lambda sec: slice(sec * g + s * w, sec * g + (s + 1) * w)

        p = zc_ref[:, col(1)] * zc_ref[:, col(2)]
        ph = halo_of(ic_ref, hc_ref, col(1)) * halo_of(ic_ref, hc_ref, col(2))
        conv = _prev_rows(p, ph, 2) * cw[0:1, ln] + _prev_rows(p, ph, 1) * cw[1:2, ln] + p * cw[2:3, ln]
        yconv_o[:, ln] = (zc_ref[:, col(0)] * conv).astype(BF16)

        def shifted(sec):
            z = zr_ref[:, col(sec)]
            return z + (_prev_rows(z, halo_of(ir_ref, hr_ref, col(sec)), 1) - z) * mur_ref[:, col(sec)]

        r, k, v = shifted(0), shifted(1), shifted(2)
        log_w = -DECAY_SCALE * jax.nn.sigmoid(w0_ref[:, ln] + _dot(wd, wdec_ref[:, ln]))
        a = jax.nn.sigmoid(a0_ref[:, ln] + _dot(ad, aup_ref[:, ln]))
        gate_o[:, ln] = _dot(gd, gup_ref[:, ln]).astype(BF16)

        kk = k * kk_ref[:, ln]
        kk = kk * lax.rsqrt(jnp.maximum(_dot((kk * kk).astype(BF16), ones_bd), 1e-24))
        k2 = k * (1.0 + (a - 1.0) * ka_ref[:, ln])
        bonus_o[:, ln] = (_dot((r * k2 * rk_ref[:, ln]).astype(BF16), ones_bd) * v).astype(BF16)
        bv = kk * a

        cum = _split_dot(tri, log_w)
        w_inc = jnp.exp(cum)
        w_inv = jnp.exp(-cum)
        for c in range(w_o.shape[0]):
            w_o[c, :, ln] = w_inc[(c + 1) * CHUNK - 1:(c + 1) * CHUNK, :]
        at_o[:, ln] = (-kk * jnp.exp(cum - log_w)).astype(BF16)
        rt_o[:, ln] = (r * w_inc).astype(BF16)
        bt_o[:, ln] = (bv * w_inv).astype(BF16)
        kt_o[:, ln] = (k2 * w_inv).astype(BF16)
        v_o[:, ln] = v.astype(BF16)


def _prep(z, z_init, convw, mur, mul, w0, a0, k_k, k_a, r_k, wdec, aup, gup, ones_bd, tm):
    bsz, seq = z.shape[:2]
    g3 = 3 * GROUP_W
    lora_blk = 2 * g3 // LORA_PAD
    t_of = jnp.arange(tm)
    same_chunk = (t_of[:, None] // CHUNK) == (t_of[None, :] // CHUNK)
    tri = jnp.logical_and(same_chunk, t_of[None, :] <= t_of[:, None]).astype(BF16)

    halo_idx = lambda i: jnp.maximum(i * (tm // HALO) - 1, 0)
    row = lambda w, c: pl.BlockSpec((None, tm, w), lambda b, i: (b, i, c))
    halo = lambda w, c: pl.BlockSpec((None, HALO, w), lambda b, i: (b, halo_idx(i), c))
    init = lambda w, c: pl.BlockSpec((HALO, w), lambda b, i: (0, c))
    full = lambda a: pl.BlockSpec(a.shape, lambda b, i: (0,) * a.ndim)
    consts = (convw, mur, mul, w0, a0, k_k, k_a, r_k, wdec, aup, gup, ones_bd, tri)
    out_bf = jax.ShapeDtypeStruct((bsz, seq, GROUP_W), BF16)
    out_spec = pl.BlockSpec((None, tm, GROUP_W), lambda b, i: (b, i, 0))
    w_shape = jax.ShapeDtypeStruct((bsz, seq // CHUNK, 1, GROUP_W), F32)
    w_spec = pl.BlockSpec((None, tm // CHUNK, 1, GROUP_W), lambda b, i: (b, i, 0, 0))
    return pl.pallas_call(
        _prep_kernel,
        grid=(bsz, seq // tm),
        in_specs=[row(g3, 0), row(g3, 1), row(LORA_PAD, lora_blk),
                  halo(g3, 0), halo(g3, 1), halo(LORA_PAD, lora_blk),
                  init(g3, 0), init(g3, 1), init(LORA_PAD, lora_blk)] + [full(c) for c in consts],
        out_specs=[out_spec] * 6 + [w_spec] + [out_spec] * 2,
        out_shape=[out_bf] * 6 + [w_shape] + [out_bf] * 2,
        compiler_params=_params(("parallel", "arbitrary")),
        name="prep",
    )(z, z, z, z, z, z, z_init, z_init, z_init, *consts)


def _block_diag(x, mask):
    return jnp.where(mask, jnp.concatenate([x] * HEADS_PER_GROUP, axis=0), 0.0).astype(BF16)


def _scan_chain(chain_id, chunk, ready, rows, lanes, at_ref, rt_ref, bt_ref, kt_ref, v_ref, w_ref,
                bonus_ref, gate_ref, lnw_ref, lnb_ref, y_ref, state_ref):
    at, rt, bt, kt, v = (ref[rows, lanes] for ref in (at_ref, rt_ref, bt_ref, kt_ref, v_ref))

    shape_bd = (SCAN_W, SCAN_W)
    bd_mask = (lax.broadcasted_iota(jnp.int32, shape_bd, 0) // HEAD_N
               == lax.broadcasted_iota(jnp.int32, shape_bd, 1) // HEAD_N)
    shape_c = (CHUNK, SCAN_W)
    t_idx = lax.broadcasted_iota(jnp.int32, shape_c, 0)
    s_idx = lax.broadcasted_iota(jnp.int32, shape_c, 1) % CHUNK
    bd = lambda x: _block_diag(x.astype(F32), bd_mask)
    stack = lambda a, b: jnp.concatenate([a.astype(BF16), b.astype(BF16)], axis=0)

    ar = stack(at, rt)
    ab = _dot_nt(ar, bd(bt))
    ak = _dot_nt(ar, bd(kt))
    yield
    a_ab = jnp.where(s_idx < t_idx, ab[:CHUNK], 0.0)
    a_rb = jnp.where(s_idx <= t_idx, ab[CHUNK:], 0.0).astype(BF16)
    a_k = stack(jnp.where(s_idx < t_idx, ak[:CHUNK], 0.0), jnp.where(s_idx <= t_idx, ak[CHUNK:], 0.0))

    inv = jnp.where(s_idx == t_idx, 1.0, 0.0) + a_ab
    pw = _dot(a_ab.astype(BF16), bd(a_ab))
    yield
    for _ in range(int(math.log2(CHUNK)) - 2):
        both = _dot(stack(pw, inv), bd(pw))
        yield
        pw = both[:CHUNK]
        inv = inv + both[CHUNK:]
    inv = inv + _dot(inv.astype(BF16), bd(pw))
    assert chunk == 0 or (chain_id, chunk - 1) in ready
    state = state_ref[...]
    ars = _dot_nt(ar, state.astype(BF16))
    akv = _dot(a_k, bd(v))
    yield

    u = _dot(inv.astype(BF16), bd(ars[:CHUNK] + akv[:CHUNK]))
    yield
    y = ars[CHUNK:] + akv[CHUNK:] + _dot(a_rb, bd(u))
    state_ref[...] = (state + jnp.where(bd_mask, _dot_tn(stack(u, v), stack(bt, kt)), 0.0)) * w_ref[chunk, :, lanes]
    ready.add((chain_id, chunk))
    yield

    ones_bd = jnp.where(bd_mask, 1.0, 0.0).astype(BF16)
    mean = _dot(y.astype(BF16), ones_bd) * (1.0 / HEAD_N)
    yield
    yc = y - mean
    var = _dot((yc * yc).astype(BF16), ones_bd) * (1.0 / HEAD_N)
    yield
    out = yc * lax.rsqrt(var + GN_EPS) * lnw_ref[:, lanes] + lnb_ref[:, lanes] + bonus_ref[rows, lanes]
    y_ref[rows, lanes] = (out * gate_ref[rows, lanes]).astype(BF16)


def _scan_kernel(*refs):
    seq_refs, (lnw_ref, lnb_ref, state0_ref, y_ref, state_out_ref, state_ref) = refs[:8], refs[8:]

    @pl.when(pl.program_id(0) == 0)
    def _():
        state_ref[...] = state0_ref[...]

    bsz, rows_per_step = y_ref.shape[:2]
    ready = set()

    def chunk_chains(c):
        rows = slice(c * CHUNK, (c + 1) * CHUNK)
        return [_scan_chain(b * N_GROUPS + g, c, ready, rows, slice(g * SCAN_W, (g + 1) * SCAN_W),
                            *[r.at[b] for r in seq_refs], lnw_ref, lnb_ref, y_ref.at[b],
                            state_ref.at[b * N_GROUPS + g])
                for b in range(bsz) for g in range(N_GROUPS)]

    pending = [chunk_chains(c) for c in range(rows_per_step // CHUNK)]
    chains, stage = [], 0
    while chains or pending:
        if pending and stage % SCAN_STAGGER == 0:
            chains = chains + pending.pop(0)
        chains = [c for c in chains if next(c, True) is None]
        stage += 1

    @pl.when(pl.program_id(0) == pl.num_programs(0) - 1)
    def _():
        state_out_ref[...] = state_ref[...]


def _scan(seq_arrays, lnw, lnb, state0):
    bsz, seq = seq_arrays[0].shape[:2]
    rows = SCAN_CHUNKS_PER_STEP * CHUNK if seq % (SCAN_CHUNKS_PER_STEP * CHUNK) == 0 else CHUNK
    blk = pl.BlockSpec((bsz, rows, GROUP_W), lambda c: (0, c, 0))
    wblk = pl.BlockSpec((bsz, rows // CHUNK, 1, GROUP_W), lambda c: (0, c, 0, 0))
    vec = pl.BlockSpec((1, GROUP_W), lambda c: (0, 0))
    st = pl.BlockSpec(state0.shape, lambda c: (0, 0, 0))
    return pl.pallas_call(
        _scan_kernel,
        grid=(seq // rows,),
        in_specs=[blk] * 5 + [wblk] + [blk] * 2 + [vec, vec, st],
        out_specs=[blk, st],
        out_shape=[jax.ShapeDtypeStruct((bsz, seq, GROUP_W), BF16), jax.ShapeDtypeStruct(state0.shape, F32)],
        scratch_shapes=[pltpu.VMEM(state0.shape, F32)],
        compiler_params=_params(("arbitrary",)),
        name="scan",
    )(*seq_arrays, lnw, lnb, state0)


def _outproj_kernel(yc_ref, yr_ref, h_ref, w_ref, gpost_ref, gpre_ref, *rest, cast):
    if cast:
        h1_ref, n2_ref, wb_ref = rest
        wb_ref[...] = w_ref[...].astype(BF16)
    else:
        wd_ref, h1_ref, n2_ref, wdb_ref = rest
        wb_ref = w_ref
        wdb_ref[...] = wd_ref[...].astype(BF16)

    sub = min(OUT_ROWS, yc_ref.shape[0])
    for r0 in range(0, yc_ref.shape[0], sub):
        rows = slice(r0, r0 + sub)
        mix = _dot(yc_ref[rows, :], wb_ref[:GROUP_W, :]) + _dot(yr_ref[rows, :], wb_ref[GROUP_W:, :])
        h1 = h_ref[rows, :] + _rms(mix, gpost_ref[...])
        h1_ref[rows, :] = h1
        n2_ref[rows, :] = _rms(h1, gpre_ref[...]).astype(BF16)


def _outproj(yc, yr, h2d, w, gpost, gpre, tm, w_down=None):
    cast = w_down is None
    m = h2d.shape[0]
    n = m // tm
    assert not cast or n == 1
    row_spec = pl.BlockSpec((tm, D_MODEL), lambda i: (i, 0))
    out_shape = [jax.ShapeDtypeStruct((m, D_MODEL), F32), jax.ShapeDtypeStruct((m, D_MODEL), BF16)]
    if cast:
        extra_in, extra_args = [], ()
        extra_out = pl.BlockSpec((D_MODEL, D_MODEL), lambda i: (0, 0))
        extra_shape = jax.ShapeDtypeStruct((D_MODEL, D_MODEL), BF16)
    else:
        slab = pl.BlockSpec((D_FF // n, D_MODEL), lambda i: (i, 0))
        extra_in, extra_args = [slab], (w_down,)
        extra_out, extra_shape = slab, jax.ShapeDtypeStruct((D_FF, D_MODEL), BF16)
    return pl.pallas_call(
        functools.partial(_outproj_kernel, cast=cast),
        grid=(m // tm,),
        in_specs=[
            pl.BlockSpec((tm, GROUP_W), lambda i: (i, 0)),
            pl.BlockSpec((tm, GROUP_W), lambda i: (i, 0)),
            pl.BlockSpec((tm, D_MODEL), lambda i: (i, 0)),
            pl.BlockSpec((D_MODEL, D_MODEL), lambda i: (0, 0), pipeline_mode=pl.Buffered(1)),
            pl.BlockSpec((1, D_MODEL), lambda i: (0, 0)),
            pl.BlockSpec((1, D_MODEL), lambda i: (0, 0)),
        ] + extra_in,
        out_specs=[row_spec, row_spec, extra_out],
        out_shape=out_shape + [extra_shape],
        compiler_params=_params(("parallel",)),
        name="outproj",
    )(yc, yr, h2d, w, gpost, gpre, *extra_args)


def _ffn_up_kernel(n2_ref, wg_ref, wu_ref, cw_ref, n2pre_ref, act_ref, wgb_ref, wub_ref, carry0_ref, carry_ref):
    b, t = pl.program_id(1), pl.program_id(2)

    @pl.when(jnp.logical_and(b == 0, t == 0))
    def _():
        wgb_ref[...] = wg_ref[...].astype(BF16)
        wub_ref[...] = wu_ref[...].astype(BF16)
        pre_gate = _dot(n2pre_ref[...], wgb_ref[...])
        carry0_ref[...] = pre_gate[pre_gate.shape[0] - HALO:, :]

    @pl.when(t == 0)
    def _():
        carry_ref[...] = carry0_ref[...]

    cw = cw_ref[...]
    for r0 in range(0, n2_ref.shape[0], UP_ROWS):
        rows = slice(r0, r0 + UP_ROWS)
        x = n2_ref[rows, :]
        gate = _dot(x, wgb_ref[...])
        up = _dot(x, wub_ref[...])
        xs = jnp.concatenate([carry_ref[...], gate], axis=0)
        carry_ref[...] = gate[UP_ROWS - HALO:, :]
        gc = pltpu.roll(xs, 2, 0)[HALO:] * cw[0:1] + pltpu.roll(xs, 1, 0)[HALO:] * cw[1:2] + gate * cw[2:3]
        act_ref[rows, :] = (gc * jax.nn.sigmoid(gc) * up).astype(BF16)


def _ffn_up(n2, w_gate_up, conv_ffn, n2_pre):
    bsz, seq = n2.shape[:2]
    n_tiles = D_FF // TN_UP
    return pl.pallas_call(
        _ffn_up_kernel,
        grid=(n_tiles, bsz, seq // TM_UP),
        in_specs=[
            pl.BlockSpec((None, TM_UP, D_MODEL), lambda j, b, t: (b, t, 0)),
            pl.BlockSpec((D_MODEL, TN_UP), lambda j, b, t: (0, j)),
            pl.BlockSpec((D_MODEL, TN_UP), lambda j, b, t: (0, j + n_tiles)),
            pl.BlockSpec((3, TN_UP), lambda j, b, t: (0, j)),
            pl.BlockSpec(n2_pre.shape, lambda j, b, t: (0, 0)),
        ],
        out_specs=pl.BlockSpec((None, TM_UP, TN_UP), lambda j, b, t: (b, t, j)),
        out_shape=jax.ShapeDtypeStruct((bsz, seq, D_FF), BF16),
        scratch_shapes=[pltpu.VMEM((D_MODEL, TN_UP), BF16), pltpu.VMEM((D_MODEL, TN_UP), BF16),
                        pltpu.VMEM((HALO, TN_UP), F32), pltpu.VMEM((HALO, TN_UP), F32)],
        compiler_params=_params(("arbitrary", "arbitrary", "arbitrary")),
        name="ffn_up",
    )(n2, w_gate_up, w_gate_up, conv_ffn, n2_pre)


def _ffn_down_kernel(act_ref, w_ref, h1_ref, g_ref, out_ref):
    k, last = pl.program_id(1), pl.num_programs(1) - 1

    @pl.when(k == 0)
    def _():
        out_ref[...] = _dot(act_ref[...], w_ref[...])

    @pl.when(jnp.logical_and(k > 0, k < last))
    def _():
        out_ref[...] += _dot(act_ref[...], w_ref[...])

    @pl.when(k == last)
    def _():
        for r0 in range(0, out_ref.shape[0], DOWN_ROWS):
            rows = slice(r0, r0 + DOWN_ROWS)
            f = out_ref[rows, :] + _dot(act_ref[rows, :], w_ref[...])
            out_ref[rows, :] = h1_ref[rows, :] + _rms(f, g_ref[...])


def _ffn_down(act, w, h1, g):
    m = act.shape[0]
    tm = TM_DOWN
    return pl.pallas_call(
        _ffn_down_kernel,
        grid=(m // tm, D_FF // TK_DOWN),
        in_specs=[
            pl.BlockSpec((tm, TK_DOWN), lambda i, k: (i, k)),
            pl.BlockSpec((TK_DOWN, D_MODEL), lambda i, k: (k, 0)),
            pl.BlockSpec((tm, D_MODEL), lambda i, k: (i, 0)),
            pl.BlockSpec((1, D_MODEL), lambda i, k: (0, 0)),
        ],
        out_specs=pl.BlockSpec((tm, D_MODEL), lambda i, k: (i, 0)),
        out_shape=jax.ShapeDtypeStruct((m, D_MODEL), F32),
        compiler_params=_params(("parallel", "arbitrary")),
        name="ffn_down",
    )(act, w, h1, g)


def kernel(x, meta_tokens, g_pre_mix, w_in, conv_w, mu_rwkv, w_decay_up, w0, a_up, a0, g_up, k_k, k_a,
           r_k, ln_x_w, ln_x_b, w_out, g_post_mix, g_pre_ffn, w_gate_up, conv_ffn, w_down, g_post_ffn):
    bsz, seq, _ = x.shape
    assert x.shape[2] == D_MODEL and meta_tokens.shape == (N_META, D_MODEL)
    row = lambda p: p.reshape(1, -1).astype(F32)
    g3 = 3 * GROUP_W

    def window_rows(w, start, win):
        return jnp.pad(w, ((start - win[0], win[1] - start - w.shape[0]), (0, 0))).astype(BF16)

    mu_r, mu_l = row(mu_rwkv[:g3]), row(jnp.pad(mu_rwkv[g3:], (0, LORA_PAD - LORA_W)))
    wdec = window_rows(w_decay_up, 0, WD_WIN)
    aup = window_rows(a_up, DECAY_LORA, AD_WIN)
    gup = window_rows(g_up, DECAY_LORA + AAA_LORA, GD_WIN)
    head_of = jnp.arange(PREP_SLAB) // HEAD_N
    ones_bd = (head_of[:, None] == head_of[None, :]).astype(BF16)
    prep_consts = (conv_w.astype(F32), mu_r, mu_l, row(w0), row(a0), row(k_k), row(k_a), row(r_k),
                   wdec, aup, gup, ones_bd)
    gpre, gpost, gffn = row(g_pre_mix), row(g_post_mix), row(g_pre_ffn)
    lnw, lnb = row(ln_x_w), row(ln_x_b)

    pre = jnp.concatenate([jnp.zeros((PREFIX - N_META, D_MODEL), F32), meta_tokens.astype(F32)], axis=0)
    z_pre, w_in_b = _inproj(pre, gpre, w_in.T, PREFIX, TN_PROJ_PRE, cast=True)
    pre_out = _prep(z_pre[None], jnp.zeros((HALO, IN_COLS_PAD), F32), *prep_consts, PREFIX)
    y_pre, state_pre = _scan(pre_out[1:], lnw, lnb, jnp.zeros((N_GROUPS, SCAN_W, SCAN_W), F32))
    _, n2_pre, w_out_b = _outproj(pre_out[0][0], y_pre[0], pre, w_out, gpost, gffn, PREFIX)

    m = bsz * seq
    x2d = x.reshape(m, D_MODEL)
    z = _inproj(x2d, gpre, w_in_b, TM_PROJ, TN_PROJ, cast=False).reshape(bsz, seq, IN_COLS_PAD)
    main_out = _prep(z, z_pre[PREFIX - HALO:], *prep_consts, TM_PREP)
    y_rwkv, _ = _scan(main_out[1:], lnw, lnb, jnp.tile(state_pre, (bsz, 1, 1)))
    h1, n2, w_down_b = _outproj(main_out[0].reshape(m, GROUP_W), y_rwkv.reshape(m, GROUP_W), x2d, w_out_b, gpost,
                                gffn, TM_OUT, w_down)
    act = _ffn_up(n2.reshape(bsz, seq, D_MODEL), w_gate_up, conv_ffn.astype(F32), n2_pre[PREFIX - N_META:])
    out = _ffn_down(act.reshape(m, D_FF), w_down_b, h1, row(g_post_ffn))
    return out.reshape(bsz, seq, D_MODEL)
```

```python
import functools
import math

import jax
import jax.numpy as jnp
from jax import lax
from jax.experimental import pallas as pl
from jax.experimental.pallas import tpu as pltpu

F32 = jnp.float32
BF16 = jnp.bfloat16

D_MODEL = 2048
N_META = 16
GROUP_W = 1024
HEAD_N = 64
DECAY_LORA = 96
AAA_LORA = 96
GATE_LORA = 256
LANE = 128
LORA_W = DECAY_LORA + AAA_LORA + GATE_LORA
LORA_PAD = 512
WD_WIN = (0, LANE)
AD_WIN = (0, 2 * LANE)
GD_WIN = (LANE, LORA_PAD)
IN_COLS = 3 * GROUP_W + 3 * GROUP_W + LORA_W
IN_COLS_PAD = IN_COLS - LORA_W + LORA_PAD
D_FF = 5632
RMS_EPS = 1e-6
GN_EPS = 64e-5
DECAY_SCALE = math.exp(-0.5)

CHUNK = 64
assert CHUNK == HEAD_N
PREFIX = CHUNK
HALO = 8
HEADS_PER_GROUP = 4
SCAN_W = HEADS_PER_GROUP * HEAD_N
N_GROUPS = GROUP_W // SCAN_W
SCAN_CHUNKS_PER_STEP = 4
SCAN_STAGGER = 2

VMEM_LIMIT = 56 * 1024 * 1024

TM_PROJ = 1024
TN_PROJ = IN_COLS_PAD // 4
TN_PROJ_PRE = 512
TM_PREP = 512
PREP_SLAB = 256
TM_OUT = 512
OUT_ROWS = 128
TM_UP = 2048
UP_ROWS = 1024
TN_UP = 512
TM_DOWN = 512
TK_DOWN = D_FF // 2
DOWN_ROWS = 256
assert D_FF // TK_DOWN > 1
PROJ_ROWS = 256


def _dot(a, b):
    return jnp.dot(a, b, preferred_element_type=F32)


def _dot_nt(a, b):
    return lax.dot_general(a, b, (((1,), (1,)), ((), ())), preferred_element_type=F32)


def _dot_tn(a, b):
    return lax.dot_general(a, b, (((0,), (0,)), ((), ())), preferred_element_type=F32)


def _rms(x, g):
    return x * lax.rsqrt(jnp.mean(x * x, axis=-1, keepdims=True) + RMS_EPS) * g


def _params(sem):
    return pltpu.CompilerParams(dimension_semantics=sem, vmem_limit_bytes=VMEM_LIMIT)


def _inproj_kernel(h_ref, g_ref, w_ref, z_ref, *rest, cast):
    xn_ref = rest[-1]
    j = pl.program_id(1)

    if cast:
        @pl.when(j == 0)
        def _():
            xn_ref[...] = _rms(h_ref[...], g_ref[...]).astype(BF16)

        tn = w_ref.shape[0]
        col = j * tn + lax.broadcasted_iota(jnp.int32, (tn, 1), 0)
        rest[0][...] = jnp.where(col < IN_COLS, w_ref[...], 0.0).T.astype(BF16)
        z_ref[...] = _dot(xn_ref[...], rest[0][...])
        return

    @pl.when(j == 0)
    def _():
        for r0 in range(0, h_ref.shape[0], PROJ_ROWS):
            rows = slice(r0, r0 + PROJ_ROWS)
            xn_ref[rows, :] = _rms(h_ref[rows, :], g_ref[...]).astype(BF16)
            z_ref[rows, :] = _dot(xn_ref[rows, :], w_ref[...])

    @pl.when(j > 0)
    def _():
        z_ref[...] = _dot(xn_ref[...], w_ref[...])


def _inproj(h2d, g, w, tm, tn, cast):
    m = h2d.shape[0]
    z_spec = pl.BlockSpec((tm, tn), lambda i, j: (i, j))
    z_shape = jax.ShapeDtypeStruct((m, IN_COLS_PAD), F32)
    w_spec = pl.BlockSpec((D_MODEL, tn), lambda i, j: (0, j))
    return pl.pallas_call(
        functools.partial(_inproj_kernel, cast=cast),
        grid=(m // tm, IN_COLS_PAD // tn),
        in_specs=[
            pl.BlockSpec((tm, D_MODEL), lambda i, j: (i, 0)),
            pl.BlockSpec((1, D_MODEL), lambda i, j: (0, 0)),
            pl.BlockSpec((tn, D_MODEL), lambda i, j: (j, 0)) if cast else w_spec,
        ],
        out_specs=[z_spec, w_spec] if cast else z_spec,
        out_shape=[z_shape, jax.ShapeDtypeStruct((D_MODEL, IN_COLS_PAD), BF16)] if cast else z_shape,
        scratch_shapes=[pltpu.VMEM((tm, D_MODEL), BF16)],
        compiler_params=_params(("parallel", "arbitrary")),
        name="inproj",
    )(h2d, g, w)


def _prev_rows(cur, halo, shift):
    xs = jnp.concatenate([halo, cur], axis=0)
    return pltpu.roll(xs, shift, 0)[HALO:]


def _split_dot(m01, x):
    x1 = x.astype(BF16)
    r1 = x - x1.astype(F32)
    x2 = r1.astype(BF16)
    x3 = (r1 - x2.astype(F32)).astype(BF16)
    return _dot(m01, x1) + _dot(m01, x2) + _dot(m01, x3)


def _prep_kernel(zc_ref, zr_ref, zl_ref, hc_ref, hr_ref, hl_ref, ic_ref, ir_ref, il_ref,
                 convw_ref, mur_ref, mul_ref, w0_ref, a0_ref, kk_ref, ka_ref, rk_ref,
                 wdec_ref, aup_ref, gup_ref, ones_ref, tri_ref,
                 yconv_o, at_o, rt_o, bt_o, kt_o, v_o, w_o, bonus_o, gate_o):
    first = pl.program_id(1) == 0
    g, w = GROUP_W, PREP_SLAB
    halo_of = lambda init_ref, halo_ref, c: jnp.where(first, init_ref[:, c], halo_ref[:, c])

    zl = zl_ref[...]
    zsl = zl + (_prev_rows(zl, halo_of(il_ref, hl_ref, slice(None)), 1) - zl) * mul_ref[...]
    wd = jnp.tanh(zsl[:, WD_WIN[0]:WD_WIN[1]]).astype(BF16)
    ad = zsl[:, AD_WIN[0]:AD_WIN[1]].astype(BF16)
    gd = jax.nn.sigmoid(zsl[:, GD_WIN[0]:GD_WIN[1]]).astype(BF16)
    ones_bd = ones_ref[...]
    tri = tri_ref[...]
    cw = convw_ref[...]

    for s in range(g // w):
        ln = slice(s * w, (s + 1) * w)
        col = lambda sec: slice(sec * g + s * w, sec * g + (s + 1) * w)

        p = zc_ref[:, col(1)] * zc_ref[:, col(2)]
        ph = halo_of(ic_ref, hc_ref, col(1)) * halo_of(ic_ref, hc_ref, col(2))
        conv = _prev_rows(p, ph, 2) * cw[0:1, ln] + _prev_rows(p, ph, 1) * cw[1:2, ln] + p * cw[2:3, ln]
        yconv_o[:, ln] = (zc_ref[:, col(0)] * conv).astype(BF16)

        def shifted(sec):
            z = zr_ref[:, col(sec)]
            return z + (_prev_rows(z, halo_of(ir_ref, hr_ref, col(sec)), 1) - z) * mur_ref[:, col(sec)]

        r, k, v = shifted(0), shifted(1), shifted(2)
        log_w = -DECAY_SCALE * jax.nn.sigmoid(w0_ref[:, ln] + _dot(wd, wdec_ref[:, ln]))
        a = jax.nn.sigmoid(a0_ref[:, ln] + _dot(ad, aup_ref[:, ln]))
        gate_o[:, ln] = _dot(gd, gup_ref[:, ln]).astype(BF16)

        kk = k * kk_ref[:, ln]
        kk = kk * lax.rsqrt(jnp.maximum(_dot((kk * kk).astype(BF16), ones_bd), 1e-24))
        k2 = k * (1.0 + (a - 1.0) * ka_ref[:, ln])
        bonus_o[:, ln] = (_dot((r * k2 * rk_ref[:, ln]).astype(BF16), ones_bd) * v).astype(BF16)
        bv = kk * a

        cum = _split_dot(tri, log_w)
        w_inc = jnp.exp(cum)
        w_inv = jnp.exp(-cum)
        for c in range(w_o.shape[0]):
            w_o[c, :, ln] = w_inc[(c + 1) * CHUNK - 1:(c + 1) * CHUNK, :]
        at_o[:, ln] = (-kk * jnp.exp(cum - log_w)).astype(BF16)
        rt_o[:, ln] = (r * w_inc).astype(BF16)
        bt_o[:, ln] = (bv * w_inv).astype(BF16)
        kt_o[:, ln] = (k2 * w_inv).astype(BF16)
        v_o[:, ln] = v.astype(BF16)


def _prep(z, z_init, convw, mur, mul, w0, a0, k_k, k_a, r_k, wdec, aup, gup, ones_bd, tm):
    bsz, seq = z.shape[:2]
    g3 = 3 * GROUP_W
    lora_blk = 2 * g3 // LORA_PAD
    t_of = jnp.arange(tm)
    same_chunk = (t_of[:, None] // CHUNK) == (t_of[None, :] // CHUNK)
    tri = jnp.logical_and(same_chunk, t_of[None, :] <= t_of[:, None]).astype(BF16)

    halo_idx = lambda i: jnp.maximum(i * (tm // HALO) - 1, 0)
    row = lambda w, c: pl.BlockSpec((None, tm, w), lambda b, i: (b, i, c))
    halo = lambda w, c: pl.BlockSpec((None, HALO, w), lambda b, i: (b, halo_idx(i), c))
    init = lambda w, c: pl.BlockSpec((HALO, w), lambda b, i: (0, c))
    full = lambda a: pl.BlockSpec(a.shape, lambda b, i: (0,) * a.ndim)
    consts = (convw, mur, mul, w0, a0, k_k, k_a, r_k, wdec, aup, gup, ones_bd, tri)
    out_bf = jax.ShapeDtypeStruct((bsz, seq, GROUP_W), BF16)
    out_spec = pl.BlockSpec((None, tm, GROUP_W), lambda b, i: (b, i, 0))
    w_shape = jax.ShapeDtypeStruct((bsz, seq // CHUNK, 1, GROUP_W), F32)
    w_spec = pl.BlockSpec((None, tm // CHUNK, 1, GROUP_W), lambda b, i: (b, i, 0, 0))
    return pl.pallas_call(
        _prep_kernel,
        grid=(bsz, seq // tm),
        in_specs=[row(g3, 0), row(g3, 1), row(LORA_PAD, lora_blk),
                  halo(g3, 0), halo(g3, 1), halo(LORA_PAD, lora_blk),
                  init(g3, 0), init(g3, 1), init(LORA_PAD, lora_blk)] + [full(c) for c in consts],
        out_specs=[out_spec] * 6 + [w_spec] + [out_spec] * 2,
        out_shape=[out_bf] * 6 + [w_shape] + [out_bf] * 2,
        compiler_params=_params(("parallel", "arbitrary")),
        name="prep",
    )(z, z, z, z, z, z, z_init, z_init, z_init, *consts)


def _block_diag(x, mask):
    return jnp.where(mask, jnp.concatenate([x] * HEADS_PER_GROUP, axis=0), 0.0).astype(BF16)


def _scan_chain(chain_id, chunk, ready, rows, lanes, at_ref, rt_ref, bt_ref, kt_ref, v_ref, w_ref,
                bonus_ref, gate_ref, lnw_ref, lnb_ref, y_ref, state_ref):
    at, rt, bt, kt, v = (ref[rows, lanes] for ref in (at_ref, rt_ref, bt_ref, kt_ref, v_ref))

    shape_bd = (SCAN_W, SCAN_W)
    bd_mask = (lax.broadcasted_iota(jnp.int32, shape_bd, 0) // HEAD_N
               == lax.broadcasted_iota(jnp.int32, shape_bd, 1) // HEAD_N)
    shape_c = (CHUNK, SCAN_W)
    t_idx = lax.broadcasted_iota(jnp.int32, shape_c, 0)
    s_idx = lax.broadcasted_iota(jnp.int32, shape_c, 1) % CHUNK
    bd = lambda x: _block_diag(x.astype(F32), bd_mask)
    stack = lambda a, b: jnp.concatenate([a.astype(BF16), b.astype(BF16)], axis=0)

    ar = stack(at, rt)
    ab = _dot_nt(ar, bd(bt))
    ak = _dot_nt(ar, bd(kt))
    yield
    a_ab = jnp.where(s_idx < t_idx, ab[:CHUNK], 0.0)
    a_rb = jnp.where(s_idx <= t_idx, ab[CHUNK:], 0.0).astype(BF16)
    a_k = stack(jnp.where(s_idx < t_idx, ak[:CHUNK], 0.0), jnp.where(s_idx <= t_idx, ak[CHUNK:], 0.0))

    inv = jnp.where(s_idx == t_idx, 1.0, 0.0) + a_ab
    pw = _dot(a_ab.astype(BF16), bd(a_ab))
    yield
    for _ in range(int(math.log2(CHUNK)) - 2):
        both = _dot(stack(pw, inv), bd(pw))
        yield
        pw = both[:CHUNK]
        inv = inv + both[CHUNK:]
    inv = inv + _dot(inv.astype(BF16), bd(pw))
    assert chunk == 0 or (chain_id, chunk - 1) in ready
    state = state_ref[...]
    ars = _dot_nt(ar, state.astype(BF16))
    akv = _dot(a_k, bd(v))
    yield

    u = _dot(inv.astype(BF16), bd(ars[:CHUNK] + akv[:CHUNK]))
    yield
    y = ars[CHUNK:] + akv[CHUNK:] + _dot(a_rb, bd(u))
    state_ref[...] = (state + jnp.where(bd_mask, _dot_tn(stack(u, v), stack(bt, kt)), 0.0)) * w_ref[chunk, :, lanes]
    ready.add((chain_id, chunk))
    yield

    ones_bd = jnp.where(bd_mask, 1.0, 0.0).astype(BF16)
    mean = _dot(y.astype(BF16), ones_bd) * (1.0 / HEAD_N)
    yield
    yc = y - mean
    var = _dot((yc * yc).astype(BF16), ones_bd) * (1.0 / HEAD_N)
    yield
    out = yc * lax.rsqrt(var + GN_EPS) * lnw_ref[:, lanes] + lnb_ref[:, lanes] + bonus_ref[rows, lanes]
    y_ref[rows, lanes] = (out * gate_ref[rows, lanes]).astype(BF16)


def _scan_kernel(*refs):
    seq_refs, (lnw_ref, lnb_ref, state0_ref, y_ref, state_out_ref, state_ref) = refs[:8], refs[8:]

    @pl.when(pl.program_id(0) == 0)
    def _():
        state_ref[...] = state0_ref[...]

    bsz, rows_per_step = y_ref.shape[:2]
    ready = set()

    def chunk_chains(c):
        rows = slice(c * CHUNK, (c + 1) * CHUNK)
        return [_scan_chain(b * N_GROUPS + g, c, ready, rows, slice(g * SCAN_W, (g + 1) * SCAN_W),
                            *[r.at[b] for r in seq_refs], lnw_ref, lnb_ref, y_ref.at[b],
                            state_ref.at[b * N_GROUPS + g])
                for b in range(bsz) for g in range(N_GROUPS)]

    pending = [chunk_chains(c) for c in range(rows_per_step // CHUNK)]
    chains, stage = [], 0
    while chains or pending:
        if pending and stage % SCAN_STAGGER == 0:
            chains = chains + pending.pop(0)
        chains = [c for c in chains if next(c, True) is None]
        stage += 1

    @pl.when(pl.program_id(0) == pl.num_programs(0) - 1)
    def _():
        state_out_ref[...] = state_ref[...]


def _scan(seq_arrays, lnw, lnb, state0):
    bsz, seq = seq_arrays[0].shape[:2]
    rows = SCAN_CHUNKS_PER_STEP * CHUNK if seq % (SCAN_CHUNKS_PER_STEP * CHUNK) == 0 else CHUNK
    blk = pl.BlockSpec((bsz, rows, GROUP_W), lambda c: (0, c, 0))
    wblk = pl.BlockSpec((bsz, rows // CHUNK, 1, GROUP_W), lambda c: (0, c, 0, 0))
    vec = pl.BlockSpec((1, GROUP_W), lambda c: (0, 0))
    st = pl.BlockSpec(state0.shape, lambda c: (0, 0, 0))
    return pl.pallas_call(
        _scan_kernel,
        grid=(seq // rows,),
        in_specs=[blk] * 5 + [wblk] + [blk] * 2 + [vec, vec, st],
        out_specs=[blk, st],
        out_shape=[jax.ShapeDtypeStruct((bsz, seq, GROUP_W), BF16), jax.ShapeDtypeStruct(state0.shape, F32)],
        scratch_shapes=[pltpu.VMEM(state0.shape, F32)],
        compiler_params=_params(("arbitrary",)),
        name="scan",
    )(*seq_arrays, lnw, lnb, state0)


def _outproj_kernel(yc_ref, yr_ref, h_ref, w_ref, gpost_ref, gpre_ref, *rest, cast):
    if cast:
        h1_ref, n2_ref, wb_ref = rest
        wb_ref[...] = w_ref[...].astype(BF16)
    else:
        wd_ref, h1_ref, n2_ref, wdb_ref = rest
        wb_ref = w_ref
        wdb_ref[...] = wd_ref[...].astype(BF16)

    sub = min(OUT_ROWS, yc_ref.shape[0])
    for r0 in range(0, yc_ref.shape[0], sub):
        rows = slice(r0, r0 + sub)
        mix = _dot(yc_ref[rows, :], wb_ref[:GROUP_W, :]) + _dot(yr_ref[rows, :], wb_ref[GROUP_W:, :])
        h1 = h_ref[rows, :] + _rms(mix, gpost_ref[...])
        h1_ref[rows, :] = h1
        n2_ref[rows, :] = _rms(h1, gpre_ref[...]).astype(BF16)


def _outproj(yc, yr, h2d, w, gpost, gpre, tm, w_down=None):
    cast = w_down is None
    m = h2d.shape[0]
    n = m // tm
    assert not cast or n == 1
    row_spec = pl.BlockSpec((tm, D_MODEL), lambda i: (i, 0))
    out_shape = [jax.ShapeDtypeStruct((m, D_MODEL), F32), jax.ShapeDtypeStruct((m, D_MODEL), BF16)]
    if cast:
        extra_in, extra_args = [], ()
        extra_out = pl.BlockSpec((D_MODEL, D_MODEL), lambda i: (0, 0))
        extra_shape = jax.ShapeDtypeStruct((D_MODEL, D_MODEL), BF16)
    else:
        slab = pl.BlockSpec((D_FF // n, D_MODEL), lambda i: (i, 0))
        extra_in, extra_args = [slab], (w_down,)
        extra_out, extra_shape = slab, jax.ShapeDtypeStruct((D_FF, D_MODEL), BF16)
    return pl.pallas_call(
        functools.partial(_outproj_kernel, cast=cast),
        grid=(m // tm,),
        in_specs=[
            pl.BlockSpec((tm, GROUP_W), lambda i: (i, 0)),
            pl.BlockSpec((tm, GROUP_W), lambda i: (i, 0)),
            pl.BlockSpec((tm, D_MODEL), lambda i: (i, 0)),
            pl.BlockSpec((D_MODEL, D_MODEL), lambda i: (0, 0), pipeline_mode=pl.Buffered(1)),
            pl.BlockSpec((1, D_MODEL), lambda i: (0, 0)),
            pl.BlockSpec((1, D_MODEL), lambda i: (0, 0)),
        ] + extra_in,
        out_specs=[row_spec, row_spec, extra_out],
        out_shape=out_shape + [extra_shape],
        compiler_params=_params(("parallel",)),
        name="outproj",
    )(yc, yr, h2d, w, gpost, gpre, *extra_args)


def _ffn_up_kernel(n2_ref, wg_ref, wu_ref, cw_ref, n2pre_ref, act_ref, wgb_ref, wub_ref, carry0_ref, carry_ref):
    b, t = pl.program_id(1), pl.program_id(2)

    @pl.when(jnp.logical_and(b == 0, t == 0))
    def _():
        wgb_ref[...] = wg_ref[...].astype(BF16)
        wub_ref[...] = wu_ref[...].astype(BF16)
        pre_gate = _dot(n2pre_ref[...], wgb_ref[...])
        carry0_ref[...] = pre_gate[pre_gate.shape[0] - HALO:, :]

    @pl.when(t == 0)
    def _():
        carry_ref[...] = carry0_ref[...]

    cw = cw_ref[...]
    for r0 in range(0, n2_ref.shape[0], UP_ROWS):
        rows = slice(r0, r0 + UP_ROWS)
        x = n2_ref[rows, :]
        gate = _dot(x, wgb_ref[...])
        up = _dot(x, wub_ref[...])
        xs = jnp.concatenate([carry_ref[...], gate], axis=0)
        carry_ref[...] = gate[UP_ROWS - HALO:, :]
        gc = pltpu.roll(xs, 2, 0)[HALO:] * cw[0:1] + pltpu.roll(xs, 1, 0)[HALO:] * cw[1:2] + gate * cw[2:3]
        act_ref[rows, :] = (gc * jax.nn.sigmoid(gc) * up).astype(BF16)


def _ffn_up(n2, w_gate_up, conv_ffn, n2_pre):
    bsz, seq = n2.shape[:2]
    n_tiles = D_FF // TN_UP
    return pl.pallas_call(
        _ffn_up_kernel,
        grid=(n_tiles, bsz, seq // TM_UP),
        in_specs=[
            pl.BlockSpec((None, TM_UP, D_MODEL), lambda j, b, t: (b, t, 0)),
            pl.BlockSpec((D_MODEL, TN_UP), lambda j, b, t: (0, j)),
            pl.BlockSpec((D_MODEL, TN_UP), lambda j, b, t: (0, j + n_tiles)),
            pl.BlockSpec((3, TN_UP), lambda j, b, t: (0, j)),
            pl.BlockSpec(n2_pre.shape, lambda j, b, t: (0, 0)),
        ],
        out_specs=pl.BlockSpec((None, TM_UP, TN_UP), lambda j, b, t: (b, t, j)),
        out_shape=jax.ShapeDtypeStruct((bsz, seq, D_FF), BF16),
        scratch_shapes=[pltpu.VMEM((D_MODEL, TN_UP), BF16), pltpu.VMEM((D_MODEL, TN_UP), BF16),
                        pltpu.VMEM((HALO, TN_UP), F32), pltpu.VMEM((HALO, TN_UP), F32)],
        compiler_params=_params(("arbitrary", "arbitrary", "arbitrary")),
        name="ffn_up",
    )(n2, w_gate_up, w_gate_up, conv_ffn, n2_pre)


def _ffn_down_kernel(act_ref, w_ref, h1_ref, g_ref, out_ref):
    k, last = pl.program_id(1), pl.num_programs(1) - 1

    @pl.when(k == 0)
    def _():
        out_ref[...] = _dot(act_ref[...], w_ref[...])

    @pl.when(jnp.logical_and(k > 0, k < last))
    def _():
        out_ref[...] += _dot(act_ref[...], w_ref[...])

    @pl.when(k == last)
    def _():
        for r0 in range(0, out_ref.shape[0], DOWN_ROWS):
            rows = slice(r0, r0 + DOWN_ROWS)
            f = out_ref[rows, :] + _dot(act_ref[rows, :], w_ref[...])
            out_ref[rows, :] = h1_ref[rows, :] + _rms(f, g_ref[...])


def _ffn_down(act, w, h1, g):
    m = act.shape[0]
    tm = TM_DOWN
    return pl.pallas_call(
        _ffn_down_kernel,
        grid=(m // tm, D_FF // TK_DOWN),
        in_specs=[
            pl.BlockSpec((tm, TK_DOWN), lambda i, k: (i, k)),
            pl.BlockSpec((TK_DOWN, D_MODEL), lambda i, k: (k, 0)),
            pl.BlockSpec((tm, D_MODEL), lambda i, k: (i, 0)),
            pl.BlockSpec((1, D_MODEL), lambda i, k: (0, 0)),
        ],
        out_specs=pl.BlockSpec((tm, D_MODEL), lambda i, k: (i, 0)),
        out_shape=jax.ShapeDtypeStruct((m, D_MODEL), F32),
        compiler_params=_params(("parallel", "arbitrary")),
        name="ffn_down",
    )(act, w, h1, g)


def kernel(x, meta_tokens, g_pre_mix, w_in, conv_w, mu_rwkv, w_decay_up, w0, a_up, a0, g_up, k_k, k_a,
           r_k, ln_x_w, ln_x_b, w_out, g_post_mix, g_pre_ffn, w_gate_up, conv_ffn, w_down, g_post_ffn):
    bsz, seq, _ = x.shape
    assert x.shape[2] == D_MODEL and meta_tokens.shape == (N_META, D_MODEL)
    row = lambda p: p.reshape(1, -1).astype(F32)
    g3 = 3 * GROUP_W

    def window_rows(w, start, win):
        return jnp.pad(w, ((start - win[0], win[1] - start - w.shape[0]), (0, 0))).astype(BF16)

    mu_r, mu_l = row(mu_rwkv[:g3]), row(jnp.pad(mu_rwkv[g3:], (0, LORA_PAD - LORA_W)))
    wdec = window_rows(w_decay_up, 0, WD_WIN)
    aup = window_rows(a_up, DECAY_LORA, AD_WIN)
    gup = window_rows(g_up, DECAY_LORA + AAA_LORA, GD_WIN)
    head_of = jnp.arange(PREP_SLAB) // HEAD_N
    ones_bd = (head_of[:, None] == head_of[None, :]).astype(BF16)
    prep_consts = (conv_w.astype(F32), mu_r, mu_l, row(w0), row(a0), row(k_k), row(k_a), row(r_k),
                   wdec, aup, gup, ones_bd)
    gpre, gpost, gffn = row(g_pre_mix), row(g_post_mix), row(g_pre_ffn)
    lnw, lnb = row(ln_x_w), row(ln_x_b)

    pre = jnp.concatenate([jnp.zeros((PREFIX - N_META, D_MODEL), F32), meta_tokens.astype(F32)], axis=0)
    z_pre, w_in_b = _inproj(pre, gpre, w_in.T, PREFIX, TN_PROJ_PRE, cast=True)
    pre_out = _prep(z_pre[None], jnp.zeros((HALO, IN_COLS_PAD), F32), *prep_consts, PREFIX)
    y_pre, state_pre = _scan(pre_out[1:], lnw, lnb, jnp.zeros((N_GROUPS, SCAN_W, SCAN_W), F32))
    _, n2_pre, w_out_b = _outproj(pre_out[0][0], y_pre[0], pre, w_out, gpost, gffn, PREFIX)

    m = bsz * seq
    x2d = x.reshape(m, D_MODEL)
    z = _inproj(x2d, gpre, w_in_b, TM_PROJ, TN_PROJ, cast=False).reshape(bsz, seq, IN_COLS_PAD)
    main_out = _prep(z, z_pre[PREFIX - HALO:], *prep_consts, TM_PREP)
    y_rwkv, _ = _scan(main_out[1:], lnw, lnb, jnp.tile(state_pre, (bsz, 1, 1)))
    h1, n2, w_down_b = _outproj(main_out[0].reshape(m, GROUP_W), y_rwkv.reshape(m, GROUP_W), x2d, w_out_b, gpost,
                                gffn, TM_OUT, w_down)
    act = _ffn_up(n2.reshape(bsz, seq, D_MODEL), w_gate_up, conv_ffn.astype(F32), n2_pre[PREFIX - N_META:])
    out = _ffn_down(act.reshape(m, D_FF), w_down_b, h1, row(g_post_ffn))
    return out.reshape(bsz, seq, D_MODEL)
```
